```python
import jax, jax.numpy as jnp
from jax import lax
import numpy as np

D_MODEL = 1024
BATCH = 8
SEQ = 4096
DEPTH = 1
DEC_BATCH = 32
DEC_SEQ = 16
PAST_LEN = 1024

CHUNK = 64
N_MEM = 256
MEM_HEADS = 4
MEM_HEAD_DIM = D_MODEL // MEM_HEADS
D_POOL = D_MODEL // 2
POOL_WINDOWS = (2, 4, 8, 16)
POOL_GROUPS = len(POOL_WINDOWS)
POOL_GW = D_POOL // POOL_GROUPS
POOL_STATE = max(POOL_WINDOWS) - 1
D_SGU = D_MODEL // 2
SGU_HEADS = 4
SGU_HW = D_SGU // SGU_HEADS
SGU_CHUNK = 128
D_FF = 2816
D_IN = D_POOL + 2 * D_SGU + 2 * D_MODEL
EPS = 1e-6

kernel_name = "hybrid_pool_sgu_streaming_step"


def _rmsnorm(x, g):
    xf = x.astype(jnp.float32)
    y = xf * lax.rsqrt(jnp.mean(xf * xf, axis=-1, keepdims=True) + EPS)
    return (y * g.astype(jnp.float32)).astype(x.dtype)


def _layernorm(x, g):
    xf = x.astype(jnp.float32)
    mu = jnp.mean(xf, axis=-1, keepdims=True)
    xc = xf - mu
    y = xc * lax.rsqrt(jnp.mean(xc * xc, axis=-1, keepdims=True) + EPS)
    return (y * g.astype(jnp.float32)).astype(x.dtype)


def _swiglu(x, w1, w3, w2):
    return (jax.nn.silu(x @ w1) * (x @ w3)) @ w2


def _sgu_mask():
    blk = jnp.arange(SGU_CHUNK) // CHUNK
    return blk[:, None] >= blk[None, :]


def _pool_mixer(xa, prefix, pos0, w_pool, pool_scale):
    B, L, _ = xa.shape
    cat = jnp.concatenate([prefix.astype(xa.dtype), xa], axis=1)
    csum = jnp.cumsum(cat.astype(jnp.float32), axis=1)
    csum = jnp.concatenate([jnp.zeros_like(csum[:, :1]), csum], axis=1)
    end = csum[:, POOL_STATE + 1:]
    pos = pos0 + jnp.arange(L)
    outs = []
    for g, w in enumerate(POOL_WINDOWS):
        sl = slice(g * POOL_GW, (g + 1) * POOL_GW)
        start = csum[:, POOL_STATE + 1 - w: POOL_STATE + 1 - w + L, sl]
        cnt = jnp.minimum(w, pos + 1).astype(jnp.float32)[None, :, None]
        outs.append((end[..., sl] - start) / cnt)
    pooled = jnp.concatenate(outs, axis=-1).astype(xa.dtype)
    d = (pooled - xa).reshape(B, L, POOL_GROUPS, POOL_GW)
    mixed = jnp.einsum('blgc,gcd->blgd', d, w_pool).reshape(B, L, D_POOL)
    return mixed * pool_scale, cat[:, -POOL_STATE:]


def _sgu_prompt(v, w_s, b_s):
    B, L, _ = v.shape
    vb = v.reshape(B, L // SGU_CHUNK, SGU_CHUNK, SGU_HEADS, SGU_HW)
    wm = jnp.where(_sgu_mask()[None], w_s, 0.0).astype(v.dtype)
    z = jnp.einsum('hpq,bnqhc->bnphc', wm, vb) + b_s.T[:, :, None]
    return z.reshape(B, L, D_SGU)


def _sgu_sample(v, w_s, b_s):
    B, T, _ = v.shape
    vb = v.reshape(B, T, SGU_HEADS, SGU_HW)
    wm = jnp.where(_sgu_mask()[None], w_s, 0.0).astype(v.dtype)[:, :T, :T]
    z = jnp.einsum('hpq,bqhc->bphc', wm, vb) + b_s[:, :T].T[:, :, None]
    return z.reshape(B, T, D_SGU)


def _mem_kv(mem, g_mem, w_mk, w_mv):
    B = mem.shape[0]
    mn = _rmsnorm(mem, g_mem)
    k = (mn @ w_mk).reshape(B, N_MEM, MEM_HEADS, MEM_HEAD_DIM)
    v = (mn @ w_mv).reshape(B, N_MEM, MEM_HEADS, MEM_HEAD_DIM)
    return k, v


def _layer(x, mem_k, mem_v, pool_prefix, pos0, is_prompt, p):
    B, L, _ = x.shape
    h = x + 0.5 * _swiglu(_rmsnorm(x, p['g_ff1']), p['w1a'], p['w3a'], p['w2a'])
    n = _rmsnorm(h, p['g_mix'])
    z = n @ p['w_in']
    xa = z[..., :D_POOL]
    uv = jax.nn.gelu(z[..., D_POOL:D_POOL + 2 * D_SGU])
    u, v = uv[..., :D_SGU], uv[..., D_SGU:]
    gate = jax.nn.sigmoid(z[..., D_POOL + 2 * D_SGU:] + p['b_gate'])
    g_a, g_b = gate[..., :D_MODEL], gate[..., D_MODEL:]
    a, pool_state = _pool_mixer(xa, pool_prefix, pos0, p['w_pool'], p['pool_scale'])
    vn = _layernorm(v, p['g_sgu'])
    s = _sgu_prompt(vn, p['w_s'], p['b_s']) if is_prompt else _sgu_sample(vn, p['w_s'], p['b_s'])
    merged = g_a * (a @ p['w_pa']) + g_b * ((u * s) @ p['w_pb'])
    h = h + merged @ p['w_o']
    q = (_rmsnorm(h, p['g_ca']) @ p['w_q']).reshape(B, L, MEM_HEADS, MEM_HEAD_DIM)
    sc = jnp.einsum('blhd,bmhd->bhlm', q, mem_k).astype(jnp.float32) * (MEM_HEAD_DIM ** -0.5)
    pr = jax.nn.softmax(sc, axis=-1).astype(x.dtype)
    o = jnp.einsum('bhlm,bmhd->blhd', pr, mem_v).reshape(B, L, D_MODEL)
    h = h + o @ p['w_co']
    h = h + 0.5 * _swiglu(_rmsnorm(h, p['g_ff2']), p['w1b'], p['w3b'], p['w2b'])
    return h, pool_state, vn


def setup_inputs(seed: int = 0) -> dict:
    ks = iter(jax.random.split(jax.random.key(seed), 64))
    nrm = lambda shape, scale: jax.random.normal(next(ks), shape, jnp.float32) * scale
    gain = lambda shape: 1.0 + 0.1 * jax.random.normal(next(ks), shape, jnp.float32)
    Lr = DEPTH
    return {
        "x_prompt": nrm((BATCH, SEQ, D_MODEL), 1.0),
        "x_sample": nrm((DEC_BATCH, DEC_SEQ, D_MODEL), 1.0),
        "state_pool": nrm((Lr, DEC_BATCH, POOL_STATE, D_POOL), 1.0),
        "cache_mem_k": nrm((Lr, DEC_BATCH, N_MEM, MEM_HEADS, MEM_HEAD_DIM), 1.0),
        "cache_mem_v": nrm((Lr, DEC_BATCH, N_MEM, MEM_HEADS, MEM_HEAD_DIM), 1.0),
        "mem_prompt": nrm((BATCH, N_MEM, D_MODEL), 1.0),
        "g_ff1": gain((Lr, D_MODEL)),
        "w1a": nrm((Lr, D_MODEL, D_FF), D_MODEL ** -0.5),
        "w3a": nrm((Lr, D_MODEL, D_FF), D_MODEL ** -0.5),
        "w2a": nrm((Lr, D_FF, D_MODEL), D_FF ** -0.5),
        "g_mix": gain((Lr, D_MODEL)),
        "w_in": nrm((Lr, D_MODEL, D_IN), D_MODEL ** -0.5),
        "b_gate": nrm((Lr, 2 * D_MODEL), 0.1),
        "w_pool": nrm((Lr, POOL_GROUPS, POOL_GW, POOL_GW), POOL_GW ** -0.5),
        "pool_scale": gain((Lr, D_POOL)),
        "g_sgu": gain((Lr, D_SGU)),
        "w_s": nrm((Lr, SGU_HEADS, SGU_CHUNK, SGU_CHUNK), SGU_CHUNK ** -0.5),
        "b_s": gain((Lr, SGU_HEADS, SGU_CHUNK)),
        "w_pa": nrm((Lr, D_POOL, D_MODEL), D_POOL ** -0.5),
        "w_pb": nrm((Lr, D_SGU, D_MODEL), D_SGU ** -0.5),
        "w_o": nrm((Lr, D_MODEL, D_MODEL), D_MODEL ** -0.5),
        "g_mem": gain((Lr, D_MODEL)),
        "w_mk": nrm((Lr, D_MODEL, D_MODEL), D_MODEL ** -0.5),
        "w_mv": nrm((Lr, D_MODEL, D_MODEL), D_MODEL ** -0.5),
        "g_ca": gain((Lr, D_MODEL)),
        "w_q": nrm((Lr, D_MODEL, D_MODEL), D_MODEL ** -0.5),
        "w_co": nrm((Lr, D_MODEL, D_MODEL), D_MODEL ** -0.5),
        "g_ff2": gain((Lr, D_MODEL)),
        "w1b": nrm((Lr, D_MODEL, D_FF), D_MODEL ** -0.5),
        "w3b": nrm((Lr, D_MODEL, D_FF), D_MODEL ** -0.5),
        "w2b": nrm((Lr, D_FF, D_MODEL), D_FF ** -0.5),
        "g_final": gain((D_MODEL,)),
    }


def reference(x_prompt, x_sample, state_pool, cache_mem_k, cache_mem_v, mem_prompt,
              g_ff1, w1a, w3a, w2a, g_mix, w_in, b_gate, w_pool, pool_scale, g_sgu, w_s, b_s,
              w_pa, w_pb, w_o, g_mem, w_mk, w_mv, g_ca, w_q, w_co, g_ff2, w1b, w3b, w2b, g_final):
    hp, hs = x_prompt, x_sample
    zero_prefix = jnp.zeros((x_prompt.shape[0], POOL_STATE, D_POOL), x_prompt.dtype)
    pool_p_list, pool_s_list, sgu_v_list, mk_list, mv_list = [], [], [], [], []
    for l in range(DEPTH):
        p = dict(g_ff1=g_ff1[l], w1a=w1a[l], w3a=w3a[l], w2a=w2a[l], g_mix=g_mix[l], w_in=w_in[l],
                 b_gate=b_gate[l], w_pool=w_pool[l], pool_scale=pool_scale[l], g_sgu=g_sgu[l],
                 w_s=w_s[l], b_s=b_s[l], w_pa=w_pa[l], w_pb=w_pb[l], w_o=w_o[l], g_ca=g_ca[l],
                 w_q=w_q[l], w_co=w_co[l], g_ff2=g_ff2[l], w1b=w1b[l], w3b=w3b[l], w2b=w2b[l])
        mk_p, mv_p = _mem_kv(mem_prompt, g_mem[l], w_mk[l], w_mv[l])
        hp, pool_p, _ = _layer(hp, mk_p, mv_p, zero_prefix, 0, True, p)
        hs, pool_s, sgu_v = _layer(hs, cache_mem_k[l], cache_mem_v[l], state_pool[l], PAST_LEN, False, p)
        pool_p_list.append(pool_p)
        pool_s_list.append(pool_s)
        sgu_v_list.append(sgu_v)
        mk_list.append(mk_p)
        mv_list.append(mv_p)
    y_prompt = _rmsnorm(hp, g_final)
    y_sample = _rmsnorm(hs, g_final)
    pool_prompt = jnp.stack(pool_p_list)
    pool_sample = jnp.stack(pool_s_list)
    sgu_v_sample = jnp.stack(sgu_v_list)
    mem_k_prompt = jnp.stack(mk_list)
    mem_v_prompt = jnp.stack(mv_list)
    return (y_prompt, y_sample, pool_prompt, pool_sample, sgu_v_sample, mem_k_prompt, mem_v_prompt)
```

```python
import functools

import jax
import jax.numpy as jnp
from jax import lax
from jax.experimental import pallas as pl
from jax.experimental.pallas import tpu as pltpu

D_MODEL = 1024
PAST_LEN = 1024
CHUNK = 64
N_MEM = 256
MEM_HEADS = 4
MEM_HEAD_DIM = D_MODEL // MEM_HEADS
D_POOL = D_MODEL // 2
POOL_WINDOWS = (2, 4, 8, 16)
POOL_GROUPS = len(POOL_WINDOWS)
POOL_GW = D_POOL // POOL_GROUPS
POOL_STATE = max(POOL_WINDOWS) - 1
D_SGU = D_MODEL // 2
SGU_HEADS = 4
SGU_HW = D_SGU // SGU_HEADS
SGU_CHUNK = 128
D_FF = 2816
D_IN = D_POOL + 2 * D_SGU + 2 * D_MODEL
EPS = 1e-6

V7X_VMEM_LIMIT_BYTES = 56 * 1024 * 1024
F32_SUBLANES = 8
POOL_HALO = 2 * F32_SUBLANES
assert POOL_HALO >= POOL_STATE + 1

FFN_ROWS = 512
MIX_ROWS = 256

BF16 = jnp.bfloat16
F32 = jnp.float32


def _dot(a, b):
    return jnp.dot(a, b, preferred_element_type=F32)


def _rmsnorm(x, g):
    y = x * lax.rsqrt(jnp.mean(x * x, axis=-1, keepdims=True) + EPS)
    return y * g


def _layernorm(x, g):
    mu = jnp.mean(x, axis=-1, keepdims=True)
    xc = x - mu
    y = xc * lax.rsqrt(jnp.mean(xc * xc, axis=-1, keepdims=True) + EPS)
    return y * g


def _const_spec(shape):
    nd = len(shape)
    return pl.BlockSpec(shape, lambda *_: (0,) * nd, pipeline_mode=pl.Buffered(1))


def _params(*sem):
    return pltpu.CompilerParams(dimension_semantics=sem, vmem_limit_bytes=V7X_VMEM_LIMIT_BYTES)


def _mem_kv_kernel(mem_ref, g_ref, wk_ref, wv_ref, k_ref, v_ref, kb_ref, vb_ref):
    mn = _rmsnorm(mem_ref[0], g_ref[...]).astype(BF16)
    k = _dot(mn, wk_ref[...])
    v = _dot(mn, wv_ref[...])
    k_ref[0] = k
    v_ref[0] = v
    kb_ref[0] = k.astype(BF16)
    vb_ref[0] = v.astype(BF16)


def _mem_kv(mem, g, wk, wv):
    nb = mem.shape[0]
    blk = pl.BlockSpec((1, N_MEM, D_MODEL), lambda b: (b, 0, 0))
    return pl.pallas_call(
        _mem_kv_kernel,
        out_shape=(jax.ShapeDtypeStruct((nb, N_MEM, D_MODEL), F32),) * 2
        + (jax.ShapeDtypeStruct((nb, N_MEM, D_MODEL), BF16),) * 2,
        grid=(nb,),
        in_specs=[blk, _const_spec((1, D_MODEL)), _const_spec((D_MODEL, D_MODEL)),
                  _const_spec((D_MODEL, D_MODEL))],
        out_specs=(blk,) * 4,
        compiler_params=_params("arbitrary"),
        name="mem_kv",
    )(mem, g, wk, wv)


def _ffn_kernel(*refs, pre, final):
    refs = list(refs)
    x_ref = refs.pop(0)
    if pre:
        o_ref_in, wco_ref = refs.pop(0), refs.pop(0)
    g_ref, w1_ref, w3_ref, w2_ref = refs[:4]
    refs = refs[4:]
    if final:
        gf_ref = refs.pop(0)
    out_ref = refs.pop(0)

    x = x_ref[...]
    if pre:
        x = x + _dot(o_ref_in[...], wco_ref[...])
    n = _rmsnorm(x, g_ref[...]).astype(BF16)
    a = _dot(n, w1_ref[...])
    b = _dot(n, w3_ref[...])
    mid = (jax.nn.silu(a) * b).astype(BF16)
    y = x + 0.5 * _dot(mid, w2_ref[...])
    if final:
        y = _rmsnorm(y, gf_ref[...])
    out_ref[...] = y


def _ffn(x, g, w1, w3, w2, *, o=None, wco=None, g_final=None, name):
    rows = x.shape[0]
    tm = min(FFN_ROWS, rows)
    assert rows % tm == 0
    pre, final = o is not None, g_final is not None
    row_spec = pl.BlockSpec((tm, D_MODEL), lambda i: (i, 0))
    args, specs = [x], [row_spec]
    if pre:
        args += [o, wco]
        specs += [row_spec, _const_spec((D_MODEL, D_MODEL))]
    args += [g, w1, w3, w2]
    specs += [_const_spec((1, D_MODEL)), _const_spec((D_MODEL, D_FF)), _const_spec((D_MODEL, D_FF)),
              _const_spec((D_FF, D_MODEL))]
    if final:
        args.append(g_final)
        specs.append(_const_spec((1, D_MODEL)))
    return pl.pallas_call(
        functools.partial(_ffn_kernel, pre=pre, final=final),
        out_shape=jax.ShapeDtypeStruct((rows, D_MODEL), F32),
        grid=(rows // tm,),
        in_specs=specs,
        out_specs=row_spec,
        compiler_params=_params("arbitrary"),
        name=name,
    )(*args)


def _in_proj(h, g_mix, w_in, b_gate):
    n = _rmsnorm(h, g_mix).astype(BF16)
    z = _dot(n, w_in)
    xa = z[:, :D_POOL]
    uv = jax.nn.gelu(z[:, D_POOL:D_POOL + 2 * D_SGU])
    gate = jax.nn.sigmoid(z[:, D_POOL + 2 * D_SGU:] + b_gate)
    return xa, uv[:, :D_SGU], uv[:, D_SGU:], gate[:, :D_MODEL], gate[:, D_MODEL:]


def _pool_project(pooled, xa, wpool_ref, pool_scale):
    mixed = []
    for g in range(POOL_GROUPS):
        sl = slice(g * POOL_GW, (g + 1) * POOL_GW)
        d = (pooled[g] - xa[:, sl]).astype(BF16)
        mixed.append(_dot(d, wpool_ref[g]))
    return jnp.concatenate(mixed, axis=-1) * pool_scale


def _merge(h, a, us, g_a, g_b, wpa_ref, wpb_ref, wo_ref):
    merged = g_a * _dot(a.astype(BF16), wpa_ref[...]) + g_b * _dot(us.astype(BF16), wpb_ref[...])
    return h + _dot(merged.astype(BF16), wo_ref[...])


def _attention(q, k, v):
    outs = []
    for hd in range(MEM_HEADS):
        sl = slice(hd * MEM_HEAD_DIM, (hd + 1) * MEM_HEAD_DIM)
        sc = lax.dot_general(q[:, sl], k[:, sl], (((1,), (1,)), ((), ())),
                             preferred_element_type=F32) * (MEM_HEAD_DIM ** -0.5)
        e = jnp.exp(sc - jnp.max(sc, axis=-1, keepdims=True))
        p = e / jnp.sum(e, axis=-1, keepdims=True)
        outs.append(_dot(p.astype(BF16), v[:, sl]))
    return jnp.concatenate(outs, axis=-1)


def _mixer_prompt_kernel(h_ref, k_ref, v_ref, gmix_ref, win_ref, bgate_ref, wpool_ref, pscale_ref,
                         gsgu_ref, ws_ref, bst_ref, wpa_ref, wpb_ref, wo_ref, gca_ref, wq_ref, wco_ref,
                         out_ref, pool_ref, cat_ref, *, tm):
    j = pl.program_id(1)
    h = h_ref[0]
    xa, u, v, g_a, g_b = _in_proj(h, gmix_ref[...], win_ref[...], bgate_ref[...])

    @pl.when(j == 0)
    def _():
        cat_ref[0:POOL_HALO, :] = jnp.zeros((POOL_HALO, D_POOL), F32)

    cat_ref[POOL_HALO:POOL_HALO + tm, :] = xa
    pos = j * tm + lax.broadcasted_iota(jnp.int32, (tm, 1), 0)
    pooled = []
    for g, w in enumerate(POOL_WINDOWS):
        sl = slice(g * POOL_GW, (g + 1) * POOL_GW)
        acc = xa[:, sl]
        for back in range(1, w):
            acc = acc + cat_ref[POOL_HALO - back:POOL_HALO - back + tm, sl]
        cnt = jnp.minimum(w, pos + 1).astype(F32)
        pooled.append(acc / cnt)
    a = _pool_project(pooled, xa, wpool_ref, pscale_ref[...])
    cat_ref[0:POOL_HALO, :] = cat_ref[tm:tm + POOL_HALO, :]

    @pl.when(j == pl.num_programs(1) - 1)
    def _():
        pool_ref[0] = cat_ref[0:POOL_HALO, :]

    vn = _layernorm(v, gsgu_ref[...]).astype(BF16)
    blk_r = lax.broadcasted_iota(jnp.int32, (SGU_CHUNK, SGU_CHUNK), 0) // CHUNK
    blk_c = lax.broadcasted_iota(jnp.int32, (SGU_CHUNK, SGU_CHUNK), 1) // CHUNK
    n_chunks = tm // SGU_CHUNK
    s_heads = []
    for hd in range(SGU_HEADS):
        wm = jnp.where(blk_r >= blk_c, ws_ref[hd], 0.0).astype(BF16)
        cols = slice(hd * SGU_HW, (hd + 1) * SGU_HW)
        vcat = jnp.concatenate([vn[c * SGU_CHUNK:(c + 1) * SGU_CHUNK, cols] for c in range(n_chunks)], axis=-1)
        sh = _dot(wm, vcat) + bst_ref[:, hd:hd + 1]
        s_heads.append(jnp.concatenate([sh[:, c * SGU_HW:(c + 1) * SGU_HW] for c in range(n_chunks)], axis=0))
    s = jnp.concatenate(s_heads, axis=-1)

    h = _merge(h, a, u * s, g_a, g_b, wpa_ref, wpb_ref, wo_ref)

    q = _dot(_rmsnorm(h, gca_ref[...]).astype(BF16), wq_ref[...]).astype(BF16)
    o = _attention(q, k_ref[0], v_ref[0])
    out_ref[0] = h + _dot(o.astype(BF16), wco_ref[...])


def _mixer_prompt(h, kb, vb, w):
    nb, seq, _ = h.shape
    tm = MIX_ROWS
    assert seq % tm == 0 and tm % SGU_CHUNK == 0
    row_spec = pl.BlockSpec((1, tm, D_MODEL), lambda b, j: (b, j, 0))
    kv_spec = pl.BlockSpec((1, N_MEM, D_MODEL), lambda b, j: (b, 0, 0))
    consts = [w["g_mix"], w["w_in"], w["b_gate"], w["w_pool"], w["pool_scale"], w["g_sgu"], w["w_s"],
              w["b_s_t"], w["w_pa"], w["w_pb"], w["w_o"], w["g_ca"], w["w_q"], w["w_co"]]
    return pl.pallas_call(
        functools.partial(_mixer_prompt_kernel, tm=tm),
        out_shape=(jax.ShapeDtypeStruct((nb, seq, D_MODEL), F32),
                   jax.ShapeDtypeStruct((nb, POOL_HALO, D_POOL), F32)),
        grid=(nb, seq // tm),
        in_specs=[row_spec, kv_spec, kv_spec] + [_const_spec(c.shape) for c in consts],
        out_specs=(row_spec, pl.BlockSpec((1, POOL_HALO, D_POOL), lambda b, j: (b, 0, 0))),
        scratch_shapes=[pltpu.VMEM((tm + POOL_HALO, D_POOL), F32)],
        compiler_params=_params("arbitrary", "arbitrary"),
        name="mixer_prompt",
    )(h, kb, vb, *consts)


def _mixer_sample_kernel(h_ref, state_ref, gmix_ref, win_ref, bgate_ref, wpool_ref, pscale_ref,
                         gsgu_ref, wsx_ref, bsx_ref, wpa_ref, wpb_ref, wo_ref, gca_ref, wq_ref,
                         out_ref, q_ref, pool_ref, vn_ref, *, nb, t, past_len):
    rows = t * nb
    halo = POOL_STATE * nb
    h = h_ref[...]
    xa, u, v, g_a, g_b = _in_proj(h, gmix_ref[...], win_ref[...], bgate_ref[...])

    cat = jnp.concatenate([state_ref[...], xa], axis=0)
    pooled = []
    for g, w in enumerate(POOL_WINDOWS):
        sl = slice(g * POOL_GW, (g + 1) * POOL_GW)
        acc = xa[:, sl]
        for back in range(1, w):
            acc = acc + cat[halo - back * nb:halo - back * nb + rows, sl]
        pos = past_len + lax.broadcasted_iota(jnp.int32, (rows, 1), 0) // nb
        cnt = jnp.minimum(w, pos + 1).astype(F32)
        pooled.append(acc / cnt)
    a = _pool_project(pooled, xa, wpool_ref, pscale_ref[...])
    pool_ref[...] = cat[rows:rows + halo, :]

    vn = _layernorm(v, gsgu_ref[...])
    vn_ref[...] = vn
    s_rows = []
    for p in range(t):
        acc = jnp.broadcast_to(bsx_ref[p:p + 1, :], (nb, D_SGU))
        for qq in range(t):
            if p // CHUNK >= qq // CHUNK:
                acc = acc + wsx_ref[p, qq:qq + 1, :] * vn[qq * nb:(qq + 1) * nb, :]
        s_rows.append(acc)
    s = jnp.concatenate(s_rows, axis=0)

    h = _merge(h, a, u * s, g_a, g_b, wpa_ref, wpb_ref, wo_ref)
    out_ref[...] = h
    q_ref[...] = _dot(_rmsnorm(h, gca_ref[...]).astype(BF16), wq_ref[...]).astype(BF16)


def _mixer_sample(h, state, w, *, nb, t, past_len):
    rows = nb * t
    consts = [w["g_mix"], w["w_in"], w["b_gate"], w["w_pool"], w["pool_scale"], w["g_sgu"], w["w_s_x"],
              w["b_s_x"], w["w_pa"], w["w_pb"], w["w_o"], w["g_ca"], w["w_q"]]
    args = [h, state] + consts
    return pl.pallas_call(
        functools.partial(_mixer_sample_kernel, nb=nb, t=t, past_len=past_len),
        out_shape=(jax.ShapeDtypeStruct((rows, D_MODEL), F32),
                   jax.ShapeDtypeStruct((rows, D_MODEL), BF16),
                   jax.ShapeDtypeStruct((POOL_STATE * nb, D_POOL), F32),
                   jax.ShapeDtypeStruct((rows, D_SGU), F32)),
        grid=(1,),
        in_specs=[_const_spec(a.shape) for a in args],
        out_specs=(_const_spec((rows, D_MODEL)), _const_spec((rows, D_MODEL)),
                   _const_spec((POOL_STATE * nb, D_POOL)), _const_spec((rows, D_SGU))),
        compiler_params=_params("arbitrary"),
        name="mixer_sample",
    )(*args)


def _attn_sample_kernel(q_ref, k_ref, v_ref, o_ref):
    o_ref[0] = _attention(q_ref[0], k_ref[0].astype(BF16), v_ref[0].astype(BF16)).astype(BF16)


def _attn_sample(q, k, v):
    nb, t, _ = q.shape
    q_spec = pl.BlockSpec((1, t, D_MODEL), lambda b: (b, 0, 0))
    kv_spec = pl.BlockSpec((1, N_MEM, D_MODEL), lambda b: (b, 0, 0))
    return pl.pallas_call(
        _attn_sample_kernel,
        out_shape=jax.ShapeDtypeStruct((nb, t, D_MODEL), BF16),
        grid=(nb,),
        in_specs=[q_spec, kv_spec, kv_spec],
        out_specs=q_spec,
        compiler_params=_params("arbitrary"),
        name="attn_sample",
    )(q, k, v)


def kernel(x_prompt, x_sample, state_pool, cache_mem_k, cache_mem_v, mem_prompt, g_ff1, w1a, w3a, w2a,
           g_mix, w_in, b_gate, w_pool, pool_scale, g_sgu, w_s, b_s, w_pa, w_pb, w_o, g_mem, w_mk, w_mv,
           g_ca, w_q, w_co, g_ff2, w1b, w3b, w2b, g_final):
    nb, seq, _ = x_prompt.shape
    nbs, t, _ = x_sample.shape
    depth = g_ff1.shape[0]
    assert depth == 1
    l = 0
    row = lambda a: a.reshape(1, -1)
    bf = lambda a: a.astype(BF16)

    w = dict(
        g_mix=row(g_mix[l]), w_in=bf(w_in[l]), b_gate=row(b_gate[l]), w_pool=bf(w_pool[l]),
        pool_scale=row(pool_scale[l]), g_sgu=row(g_sgu[l]), w_s=w_s[l], b_s_t=b_s[l].T,
        w_pa=bf(w_pa[l]), w_pb=bf(w_pb[l]), w_o=bf(w_o[l]), g_ca=row(g_ca[l]), w_q=bf(w_q[l]),
        w_co=bf(w_co[l]),
        w_s_x=jnp.repeat(jnp.transpose(w_s[l][:, :t, :t], (1, 2, 0)), SGU_HW, axis=-1),
        b_s_x=jnp.repeat(b_s[l][:, :t].T, SGU_HW, axis=-1),
    )
    w1a_b, w3a_b, w2a_b = bf(w1a[l]), bf(w3a[l]), bf(w2a[l])
    w1b_b, w3b_b, w2b_b = bf(w1b[l]), bf(w3b[l]), bf(w2b[l])

    mk, mv, mk_b, mv_b = _mem_kv(mem_prompt, row(g_mem[l]), bf(w_mk[l]), bf(w_mv[l]))
    hp = _ffn(x_prompt.reshape(nb * seq, D_MODEL), row(g_ff1[l]), w1a_b, w3a_b, w2a_b, name="ffn1_prompt")
    hp, pool_p = _mixer_prompt(hp.reshape(nb, seq, D_MODEL), mk_b, mv_b, w)
    y_prompt = _ffn(hp.reshape(nb * seq, D_MODEL), row(g_ff2[l]), w1b_b, w3b_b, w2b_b,
                    g_final=row(g_final), name="ffn2_prompt").reshape(nb, seq, D_MODEL)

    xs = jnp.transpose(x_sample, (1, 0, 2)).reshape(t * nbs, D_MODEL)
    st = jnp.transpose(state_pool[l], (1, 0, 2)).reshape(POOL_STATE * nbs, D_POOL)
    hs = _ffn(xs, row(g_ff1[l]), w1a_b, w3a_b, w2a_b, name="ffn1_sample")
    hs, qs, pool_s, vn_s = _mixer_sample(hs, st, w, nb=nbs, t=t, past_len=PAST_LEN)
    to_stream_major = lambda a, n: jnp.transpose(a.reshape(n, nbs, a.shape[-1]), (1, 0, 2))
    hs = to_stream_major(hs, t)
    qs = to_stream_major(qs, t)
    os_ = _attn_sample(qs, cache_mem_k[l].reshape(nbs, N_MEM, D_MODEL),
                       cache_mem_v[l].reshape(nbs, N_MEM, D_MODEL))
    y_sample = _ffn(hs.reshape(nbs * t, D_MODEL), row(g_ff2[l]), w1b_b, w3b_b, w2b_b,
                    o=os_.reshape(nbs * t, D_MODEL), wco=w["w_co"], g_final=row(g_final),
                    name="ffn2_sample").reshape(nbs, t, D_MODEL)

    pool_prompt = pool_p[:, POOL_HALO - POOL_STATE:, :][None]
    pool_sample = to_stream_major(pool_s, POOL_STATE)[None]
    sgu_v_sample = to_stream_major(vn_s, t)[None]
    mem_k_prompt = mk.reshape(1, nb, N_MEM, MEM_HEADS, MEM_HEAD_DIM)
    mem_v_prompt = mv.reshape(1, nb, N_MEM, MEM_HEADS, MEM_HEAD_DIM)
    return (y_prompt, y_sample, pool_prompt, pool_sample, sgu_v_sample, mem_k_prompt, mem_v_prompt)
```

```python
import functools

import jax
import jax.numpy as jnp
from jax import lax
from jax.experimental import pallas as pl
from jax.experimental.pallas import tpu as pltpu

D_MODEL = 1024
PAST_LEN = 1024
CHUNK = 64
N_MEM = 256
MEM_HEADS = 4
MEM_HEAD_DIM = D_MODEL // MEM_HEADS
D_POOL = D_MODEL // 2
POOL_WINDOWS = (2, 4, 8, 16)
POOL_GROUPS = len(POOL_WINDOWS)
POOL_GW = D_POOL // POOL_GROUPS
POOL_STATE = max(POOL_WINDOWS) - 1
D_SGU = D_MODEL // 2
SGU_HEADS = 4
SGU_HW = D_SGU // SGU_HEADS
SGU_CHUNK = 128
D_FF = 2816
D_IN = D_POOL + 2 * D_SGU + 2 * D_MODEL
EPS = 1e-6

V7X_VMEM_LIMIT_BYTES = 56 * 1024 * 1024
F32_SUBLANES = 8
POOL_HALO = 2 * F32_SUBLANES
assert POOL_HALO >= POOL_STATE + 1

FFN_ROWS = 512
FFN_SUB_ROWS = 256
MIX_ROWS = 512
MIX_SUB_ROWS = 256

BF16 = jnp.bfloat16
F32 = jnp.float32


def _dot(a, b):
    return jnp.dot(a, b, preferred_element_type=F32)


def _rmsnorm(x, g):
    y = x * lax.rsqrt(jnp.mean(x * x, axis=-1, keepdims=True) + EPS)
    return y * g


def _layernorm(x, g):
    mu = jnp.mean(x, axis=-1, keepdims=True)
    xc = x - mu
    y = xc * lax.rsqrt(jnp.mean(xc * xc, axis=-1, keepdims=True) + EPS)
    return y * g


def _const_spec(shape):
    nd = len(shape)
    return pl.BlockSpec(shape, lambda *_: (0,) * nd, pipeline_mode=pl.Buffered(1))


def _params(*sem):
    return pltpu.CompilerParams(dimension_semantics=sem, vmem_limit_bytes=V7X_VMEM_LIMIT_BYTES)


def _mem_kv_kernel(mem_ref, g_ref, wk_ref, wv_ref, k_ref, v_ref, kt_ref, vb_ref):
    mn = _rmsnorm(mem_ref[0], g_ref[...]).astype(BF16)
    k = _dot(mn, wk_ref[...])
    v = _dot(mn, wv_ref[...])
    k_ref[0] = k
    v_ref[0] = v
    kt_ref[0] = k.T.astype(BF16)
    vb_ref[0] = v.astype(BF16)


def _mem_kv(mem, g, wk, wv):
    nb = mem.shape[0]
    blk = pl.BlockSpec((1, N_MEM, D_MODEL), lambda b: (b, 0, 0))
    blk_t = pl.BlockSpec((1, D_MODEL, N_MEM), lambda b: (b, 0, 0))
    return pl.pallas_call(
        _mem_kv_kernel,
        out_shape=(jax.ShapeDtypeStruct((nb, N_MEM, D_MODEL), F32),) * 2
        + (jax.ShapeDtypeStruct((nb, D_MODEL, N_MEM), BF16), jax.ShapeDtypeStruct((nb, N_MEM, D_MODEL), BF16)),
        grid=(nb,),
        in_specs=[blk, _const_spec((1, D_MODEL)), _const_spec((D_MODEL, D_MODEL)),
                  _const_spec((D_MODEL, D_MODEL))],
        out_specs=(blk, blk, blk_t, blk),
        compiler_params=_params("arbitrary"),
        name="mem_kv",
    )(mem, g, wk, wv)


def _run_side_by_side(stage_generators):
    live = list(stage_generators)
    while live:
        live = [g for g in live if next(g, StopIteration) is not StopIteration]


def _ffn_kernel(*refs, pre, final, sub):
    refs = list(refs)
    x_ref = refs.pop(0)
    if pre:
        o_ref_in, wco_ref = refs.pop(0), refs.pop(0)
    g_ref, w1_ref, w3_ref, w2_ref = refs[:4]
    refs = refs[4:]
    if final:
        gf_ref = refs.pop(0)
    out_ref = refs.pop(0)

    def sub_tile(r0):
        rows = slice(r0, r0 + sub)
        x = x_ref[rows, :]
        if pre:
            x = x + _dot(o_ref_in[rows, :], wco_ref[...])
        n = _rmsnorm(x, g_ref[...]).astype(BF16)
        yield
        a = _dot(n, w1_ref[...])
        yield
        b = _dot(n, w3_ref[...])
        yield
        mid = (jax.nn.silu(a) * b).astype(BF16)
        y = x + 0.5 * _dot(mid, w2_ref[...])
        if final:
            y = _rmsnorm(y, gf_ref[...])
        out_ref[rows, :] = y

    _run_side_by_side([sub_tile(r0) for r0 in range(0, x_ref.shape[0], sub)])


def _ffn(x, g, w1, w3, w2, *, o=None, wco=None, g_final=None, name):
    rows = x.shape[0]
    tm = min(FFN_ROWS, rows)
    assert rows % tm == 0
    pre, final = o is not None, g_final is not None
    row_spec = pl.BlockSpec((tm, D_MODEL), lambda i: (i, 0))
    args, specs = [x], [row_spec]
    if pre:
        args += [o, wco]
        specs += [row_spec, _const_spec((D_MODEL, D_MODEL))]
    args += [g, w1, w3, w2]
    specs += [_const_spec((1, D_MODEL)), _const_spec((D_MODEL, D_FF)), _const_spec((D_MODEL, D_FF)),
              _const_spec((D_FF, D_MODEL))]
    if final:
        args.append(g_final)
        specs.append(_const_spec((1, D_MODEL)))
    return pl.pallas_call(
        functools.partial(_ffn_kernel, pre=pre, final=final, sub=min(FFN_SUB_ROWS, tm)),
        out_shape=jax.ShapeDtypeStruct((rows, D_MODEL), F32),
        grid=(rows // tm,),
        in_specs=specs,
        out_specs=row_spec,
        compiler_params=_params("arbitrary"),
        name=name,
    )(*args)


def _in_proj(h, g_mix, w_in, b_gate):
    n = _rmsnorm(h, g_mix).astype(BF16)
    z = _dot(n, w_in)
    xa = z[:, :D_POOL]
    uv = jax.nn.gelu(z[:, D_POOL:D_POOL + 2 * D_SGU])
    gate = jax.nn.sigmoid(z[:, D_POOL + 2 * D_SGU:] + b_gate)
    return xa, uv[:, :D_SGU], uv[:, D_SGU:], gate[:, :D_MODEL], gate[:, D_MODEL:]


def _pool_project(pooled, xa, wpool_ref, pool_scale):
    mixed = []
    for g in range(POOL_GROUPS):
        sl = slice(g * POOL_GW, (g + 1) * POOL_GW)
        d = (pooled[g] - xa[:, sl]).astype(BF16)
        mixed.append(_dot(d, wpool_ref[g]))
    return jnp.concatenate(mixed, axis=-1) * pool_scale


def _merge(h, a, us, g_a, g_b, wpa_ref, wpb_ref, wo_ref):
    merged = g_a * _dot(a.astype(BF16), wpa_ref[...]) + g_b * _dot(us.astype(BF16), wpb_ref[...])
    return h + _dot(merged.astype(BF16), wo_ref[...])


def _attention_heads(q, k, v, *, k_transposed):
    for hd in range(MEM_HEADS):
        sl = slice(hd * MEM_HEAD_DIM, (hd + 1) * MEM_HEAD_DIM)
        if k_transposed:
            sc = _dot(q[:, sl], k[sl, :])
        else:
            sc = lax.dot_general(q[:, sl], k[:, sl], (((1,), (1,)), ((), ())), preferred_element_type=F32)
        sc = sc * (MEM_HEAD_DIM ** -0.5)
        e = jnp.exp(sc - jnp.max(sc, axis=-1, keepdims=True))
        p = e / jnp.sum(e, axis=-1, keepdims=True)
        yield _dot(p.astype(BF16), v[:, sl])


def _window_sums(halo, xa_g, window):
    s = jnp.concatenate([halo, xa_g], axis=0)
    step = 1
    while step < window:
        s = s + pltpu.roll(s, step, axis=0)
        step *= 2
    return s[POOL_HALO:]


def _mixer_prompt_kernel(h_ref, kt_ref, v_ref, gmix_ref, win_ref, bgate_ref, wpool_ref, pscale_ref,
                         gsgu_ref, ws_ref, bst_ref, wpa_ref, wpb_ref, wo_ref, gca_ref, wq_ref, wco_ref,
                         out_ref, pool_ref, carry_ref, *, tm, sub):
    j = pl.program_id(1)

    @pl.when(j == 0)
    def _():
        carry_ref[...] = jnp.zeros((POOL_HALO, D_POOL), F32)

    blk_r = lax.broadcasted_iota(jnp.int32, (SGU_CHUNK, SGU_CHUNK), 0) // CHUNK
    blk_c = lax.broadcasted_iota(jnp.int32, (SGU_CHUNK, SGU_CHUNK), 1) // CHUNK
    wm = [jnp.where(blk_r >= blk_c, ws_ref[hd], 0.0).astype(BF16) for hd in range(SGU_HEADS)]
    n_chunks = sub // SGU_CHUNK

    n_sub = tm // sub
    halos = [carry_ref[...]] + [None] * n_sub

    def sub_tile(i):
        r0 = i * sub
        h = h_ref[0, r0:r0 + sub, :]
        xa, u, v, g_a, g_b = _in_proj(h, gmix_ref[...], win_ref[...], bgate_ref[...])
        halos[i + 1] = xa[sub - POOL_HALO:, :]
        yield

        pos = j * tm + r0 + lax.broadcasted_iota(jnp.int32, (sub, 1), 0)
        pooled = []
        for g, w in enumerate(POOL_WINDOWS):
            sl = slice(g * POOL_GW, (g + 1) * POOL_GW)
            cnt = jnp.minimum(w, pos + 1).astype(F32)
            pooled.append(_window_sums(halos[i][:, sl], xa[:, sl], w) / cnt)
        a = _pool_project(pooled, xa, wpool_ref, pscale_ref[...])
        yield

        vn = _layernorm(v, gsgu_ref[...]).astype(BF16)
        s_heads = []
        for hd in range(SGU_HEADS):
            cols = slice(hd * SGU_HW, (hd + 1) * SGU_HW)
            vcat = jnp.concatenate([vn[c * SGU_CHUNK:(c + 1) * SGU_CHUNK, cols] for c in range(n_chunks)],
                                   axis=-1)
            sh = _dot(wm[hd], vcat) + bst_ref[:, hd:hd + 1]
            s_heads.append(jnp.concatenate([sh[:, c * SGU_HW:(c + 1) * SGU_HW] for c in range(n_chunks)],
                                           axis=0))
        s = jnp.concatenate(s_heads, axis=-1)
        yield

        h = _merge(h, a, u * s, g_a, g_b, wpa_ref, wpb_ref, wo_ref)
        yield
        q = _dot(_rmsnorm(h, gca_ref[...]).astype(BF16), wq_ref[...]).astype(BF16)
        yield
        outs = []
        for o_head in _attention_heads(q, kt_ref[0], v_ref[0], k_transposed=True):
            outs.append(o_head)
            yield
        o = jnp.concatenate(outs, axis=-1)
        out_ref[0, r0:r0 + sub, :] = h + _dot(o.astype(BF16), wco_ref[...])

    _run_side_by_side([sub_tile(i) for i in range(n_sub)])

    carry_ref[...] = halos[n_sub]

    @pl.when(j == pl.num_programs(1) - 1)
    def _():
        pool_ref[0] = halos[n_sub]


def _mixer_prompt(h, kb, vb, w):
    nb, seq, _ = h.shape
    tm, sub = MIX_ROWS, MIX_SUB_ROWS
    assert seq % tm == 0 and tm % sub == 0 and sub % SGU_CHUNK == 0
    row_spec = pl.BlockSpec((1, tm, D_MODEL), lambda b, j: (b, j, 0))
    kt_spec = pl.BlockSpec((1, D_MODEL, N_MEM), lambda b, j: (b, 0, 0))
    v_spec = pl.BlockSpec((1, N_MEM, D_MODEL), lambda b, j: (b, 0, 0))
    consts = [w["g_mix"], w["w_in"], w["b_gate"], w["w_pool"], w["pool_scale"], w["g_sgu"], w["w_s"],
              w["b_s_t"], w["w_pa"], w["w_pb"], w["w_o"], w["g_ca"], w["w_q"], w["w_co"]]
    return pl.pallas_call(
        functools.partial(_mixer_prompt_kernel, tm=tm, sub=sub),
        out_shape=(jax.ShapeDtypeStruct((nb, seq, D_MODEL), F32),
                   jax.ShapeDtypeStruct((nb, POOL_HALO, D_POOL), F32)),
        grid=(nb, seq // tm),
        in_specs=[row_spec, kt_spec, v_spec] + [_const_spec(c.shape) for c in consts],
        out_specs=(row_spec, pl.BlockSpec((1, POOL_HALO, D_POOL), lambda b, j: (b, 0, 0))),
        scratch_shapes=[pltpu.VMEM((POOL_HALO, D_POOL), F32)],
        compiler_params=_params("arbitrary", "arbitrary"),
        name="mixer_prompt",
    )(h, kb, vb, *consts)


def _mixer_sample_kernel(h_ref, state_ref, gmix_ref, win_ref, bgate_ref, wpool_ref, pscale_ref,
                         gsgu_ref, wsx_ref, bsx_ref, wpa_ref, wpb_ref, wo_ref, gca_ref, wq_ref,
                         out_ref, q_ref, pool_ref, vn_ref, *, nb, t, past_len):
    rows = t * nb
    halo = POOL_STATE * nb
    h = h_ref[...]
    xa, u, v, g_a, g_b = _in_proj(h, gmix_ref[...], win_ref[...], bgate_ref[...])

    cat = jnp.concatenate([state_ref[...], xa], axis=0)
    pooled = []
    for g, w in enumerate(POOL_WINDOWS):
        sl = slice(g * POOL_GW, (g + 1) * POOL_GW)
        acc = xa[:, sl]
        for back in range(1, w):
            acc = acc + cat[halo - back * nb:halo - back * nb + rows, sl]
        pos = past_len + lax.broadcasted_iota(jnp.int32, (rows, 1), 0) // nb
        cnt = jnp.minimum(w, pos + 1).astype(F32)
        pooled.append(acc / cnt)
    a = _pool_project(pooled, xa, wpool_ref, pscale_ref[...])
    pool_ref[...] = cat[rows:rows + halo, :]

    vn = _layernorm(v, gsgu_ref[...])
    vn_ref[...] = vn
    s_rows = []
    for p in range(t):
        acc = jnp.broadcast_to(bsx_ref[p:p + 1, :], (nb, D_SGU))
        for qq in range(t):
            if p // CHUNK >= qq // CHUNK:
                acc = acc + wsx_ref[p, qq:qq + 1, :] * vn[qq * nb:(qq + 1) * nb, :]
        s_rows.append(acc)
    s = jnp.concatenate(s_rows, axis=0)

    h = _merge(h, a, u * s, g_a, g_b, wpa_ref, wpb_ref, wo_ref)
    out_ref[...] = h
    q_ref[...] = _dot(_rmsnorm(h, gca_ref[...]).astype(BF16), wq_ref[...]).astype(BF16)


def _mixer_sample(h, state, w, *, nb, t, past_len):
    rows = nb * t
    consts = [w["g_mix"], w["w_in"], w["b_gate"], w["w_pool"], w["pool_scale"], w["g_sgu"], w["w_s_x"],
              w["b_s_x"], w["w_pa"], w["w_pb"], w["w_o"], w["g_ca"], w["w_q"]]
    args = [h, state] + consts
    return pl.pallas_call(
        functools.partial(_mixer_sample_kernel, nb=nb, t=t, past_len=past_len),
        out_shape=(jax.ShapeDtypeStruct((rows, D_MODEL), F32),
                   jax.ShapeDtypeStruct((rows, D_MODEL), BF16),
                   jax.ShapeDtypeStruct((POOL_STATE * nb, D_POOL), F32),
                   jax.ShapeDtypeStruct((rows, D_SGU), F32)),
        grid=(1,),
        in_specs=[_const_spec(a.shape) for a in args],
        out_specs=(_const_spec((rows, D_MODEL)), _const_spec((rows, D_MODEL)),
                   _const_spec((POOL_STATE * nb, D_POOL)), _const_spec((rows, D_SGU))),
        compiler_params=_params("arbitrary"),
        name="mixer_sample",
    )(*args)


def _attn_sample_kernel(q_ref, k_ref, v_ref, o_ref):
    heads = _attention_heads(q_ref[0], k_ref[0].astype(BF16), v_ref[0].astype(BF16), k_transposed=False)
    o_ref[0] = jnp.concatenate(list(heads), axis=-1).astype(BF16)


def _attn_sample(q, k, v):
    nb, t, _ = q.shape
    q_spec = pl.BlockSpec((1, t, D_MODEL), lambda b: (b, 0, 0))
    kv_spec = pl.BlockSpec((1, N_MEM, D_MODEL), lambda b: (b, 0, 0))
    return pl.pallas_call(
        _attn_sample_kernel,
        out_shape=jax.ShapeDtypeStruct((nb, t, D_MODEL), BF16),
        grid=(nb,),
        in_specs=[q_spec, kv_spec, kv_spec],
        out_specs=q_spec,
        compiler_params=_params("arbitrary"),
        name="attn_sample",
    )(q, k, v)


def kernel(x_prompt, x_sample, state_pool, cache_mem_k, cache_mem_v, mem_prompt, g_ff1, w1a, w3a, w2a,
           g_mix, w_in, b_gate, w_pool, pool_scale, g_sgu, w_s, b_s, w_pa, w_pb, w_o, g_mem, w_mk, w_mv,
           g_ca, w_q, w_co, g_ff2, w1b, w3b, w2b, g_final):
    nb, seq, _ = x_prompt.shape
    nbs, t, _ = x_sample.shape
    depth = g_ff1.shape[0]
    assert depth == 1
    l = 0
    row = lambda a: a.reshape(1, -1)
    bf = lambda a: a.astype(BF16)

    w = dict(
        g_mix=row(g_mix[l]), w_in=bf(w_in[l]), b_gate=row(b_gate[l]), w_pool=bf(w_pool[l]),
        pool_scale=row(pool_scale[l]), g_sgu=row(g_sgu[l]), w_s=w_s[l], b_s_t=b_s[l].T,
        w_pa=bf(w_pa[l]), w_pb=bf(w_pb[l]), w_o=bf(w_o[l]), g_ca=row(g_ca[l]), w_q=bf(w_q[l]),
        w_co=bf(w_co[l]),
        w_s_x=jnp.repeat(jnp.transpose(w_s[l][:, :t, :t], (1, 2, 0)), SGU_HW, axis=-1),
        b_s_x=jnp.repeat(b_s[l][:, :t].T, SGU_HW, axis=-1),
    )
    w1a_b, w3a_b, w2a_b = bf(w1a[l]), bf(w3a[l]), bf(w2a[l])
    w1b_b, w3b_b, w2b_b = bf(w1b[l]), bf(w3b[l]), bf(w2b[l])

    mk, mv, mk_t, mv_b = _mem_kv(mem_prompt, row(g_mem[l]), bf(w_mk[l]), bf(w_mv[l]))
    hp = _ffn(x_prompt.reshape(nb * seq, D_MODEL), row(g_ff1[l]), w1a_b, w3a_b, w2a_b, name="ffn1_prompt")
    hp, pool_p = _mixer_prompt(hp.reshape(nb, seq, D_MODEL), mk_t, mv_b, w)
    y_prompt = _ffn(hp.reshape(nb * seq, D_MODEL), row(g_ff2[l]), w1b_b, w3b_b, w2b_b,
                    g_final=row(g_final), name="ffn2_prompt").reshape(nb, seq, D_MODEL)

    xs = jnp.transpose(x_sample, (1, 0, 2)).reshape(t * nbs, D_MODEL)
    st = jnp.transpose(state_pool[l], (1, 0, 2)).reshape(POOL_STATE * nbs, D_POOL)
    hs = _ffn(xs, row(g_ff1[l]), w1a_b, w3a_b, w2a_b, name="ffn1_sample")
    hs, qs, pool_s, vn_s = _mixer_sample(hs, st, w, nb=nbs, t=t, past_len=PAST_LEN)
    to_stream_major = lambda a, n: jnp.transpose(a.reshape(n, nbs, a.shape[-1]), (1, 0, 2))
    hs = to_stream_major(hs, t)
    qs = to_stream_major(qs, t)
    os_ = _attn_sample(qs, cache_mem_k[l].reshape(nbs, N_MEM, D_MODEL),
                       cache_mem_v[l].reshape(nbs, N_MEM, D_MODEL))
    y_sample = _ffn(hs.reshape(nbs * t, D_MODEL), row(g_ff2[l]), w1b_b, w3b_b, w2b_b,
                    o=os_.reshape(nbs * t, D_MODEL), wco=w["w_co"], g_final=row(g_final),
                    name="ffn2_sample").reshape(nbs, t, D_MODEL)

    pool_prompt = pool_p[:, POOL_HALO - POOL_STATE:, :][None]
    pool_sample = to_stream_major(pool_s, POOL_STATE)[None]
    sgu_v_sample = to_stream_major(vn_s, t)[None]
    mem_k_prompt = mk.reshape(1, nb, N_MEM, MEM_HEADS, MEM_HEAD_DIM)
    mem_v_prompt = mv.reshape(1, nb, N_MEM, MEM_HEADS, MEM_HEAD_DIM)
    return (y_prompt, y_sample, pool_prompt, pool_sample, sgu_v_sample, mem_k_prompt, mem_v_prompt)
```

```python
import functools

import jax
import jax.numpy as jnp
from jax import lax
from jax.experimental import pallas as pl
from jax.experimental.pallas import tpu as pltpu

D_MODEL = 1024
PAST_LEN = 1024
CHUNK = 64
N_MEM = 256
MEM_HEADS = 4
MEM_HEAD_DIM = D_MODEL // MEM_HEADS
D_POOL = D_MODEL // 2
POOL_WINDOWS = (2, 4, 8, 16)
POOL_GROUPS = len(POOL_WINDOWS)
POOL_GW = D_POOL // POOL_GROUPS
POOL_STATE = max(POOL_WINDOWS) - 1
D_SGU = D_MODEL // 2
SGU_HEADS = 4
SGU_HW = D_SGU // SGU_HEADS
SGU_CHUNK = 128
D_FF = 2816
D_IN = D_POOL + 2 * D_SGU + 2 * D_MODEL
EPS = 1e-6

V7X_VMEM_LIMIT_BYTES = 56 * 1024 * 1024
F32_SUBLANES = 8
BF16_SUBLANES = 16
LANES = 128
CAST_STEPS = 16
POOL_HALO = 2 * F32_SUBLANES
assert POOL_HALO >= POOL_STATE + 1

FFN_ROWS = 512
FFN_SUB_ROWS = 256
MIX_ROWS = 512
MIX_SUB_ROWS = 256
ATTN_SAMPLE_STREAMS = 4

BF16 = jnp.bfloat16
F32 = jnp.float32


def _dot(a, b):
    return jnp.dot(a, b, preferred_element_type=F32)


def _rmsnorm(x, g):
    y = x * lax.rsqrt(jnp.mean(x * x, axis=-1, keepdims=True) + EPS)
    return y * g


def _layernorm(x, g):
    mu = jnp.mean(x, axis=-1, keepdims=True)
    xc = x - mu
    y = xc * lax.rsqrt(jnp.mean(xc * xc, axis=-1, keepdims=True) + EPS)
    return y * g


def _const_spec(shape):
    nd = len(shape)
    return pl.BlockSpec(shape, lambda *_: (0,) * nd, pipeline_mode=pl.Buffered(1))


def _params(*sem):
    return pltpu.CompilerParams(dimension_semantics=sem, vmem_limit_bytes=V7X_VMEM_LIMIT_BYTES)


def _cast_kernel(*refs):
    n = len(refs) // 2
    for src, dst in zip(refs[:n], refs[n:]):
        dst[...] = src[...].astype(BF16)


def _cast_bf16(arrays):
    for a in arrays:
        assert a.ndim == 2 and a.shape[0] % (CAST_STEPS * BF16_SUBLANES) == 0 and a.shape[1] % LANES == 0
    specs = [pl.BlockSpec((a.shape[0] // CAST_STEPS, a.shape[1]), lambda i: (i, 0)) for a in arrays]
    return pl.pallas_call(
        _cast_kernel,
        out_shape=tuple(jax.ShapeDtypeStruct(a.shape, BF16) for a in arrays),
        grid=(CAST_STEPS,),
        in_specs=specs,
        out_specs=tuple(specs),
        compiler_params=_params("arbitrary"),
        name="cast_weights",
    )(*arrays)


def _mem_kv_kernel(mem_ref, g_ref, wk_ref, wv_ref, k_ref, v_ref, kt_ref, vb_ref):
    mn = _rmsnorm(mem_ref[0], g_ref[...]).astype(BF16)
    k = _dot(mn, wk_ref[...])
    v = _dot(mn, wv_ref[...])
    for hd in range(MEM_HEADS):
        sl = slice(hd * MEM_HEAD_DIM, (hd + 1) * MEM_HEAD_DIM)
        k_ref[0, 0, :, hd, :] = k[:, sl]
        v_ref[0, 0, :, hd, :] = v[:, sl]
    kt_ref[0] = k.T.astype(BF16)
    vb_ref[0] = v.astype(BF16)


def _mem_kv(mem, g, wk, wv):
    nb = mem.shape[0]
    blk = pl.BlockSpec((1, N_MEM, D_MODEL), lambda b: (b, 0, 0))
    blk_t = pl.BlockSpec((1, D_MODEL, N_MEM), lambda b: (b, 0, 0))
    blk_heads = pl.BlockSpec((1, 1, N_MEM, MEM_HEADS, MEM_HEAD_DIM), lambda b: (0, b, 0, 0, 0))
    return pl.pallas_call(
        _mem_kv_kernel,
        out_shape=(jax.ShapeDtypeStruct((1, nb, N_MEM, MEM_HEADS, MEM_HEAD_DIM), F32),) * 2
        + (jax.ShapeDtypeStruct((nb, D_MODEL, N_MEM), BF16), jax.ShapeDtypeStruct((nb, N_MEM, D_MODEL), BF16)),
        grid=(nb,),
        in_specs=[blk, _const_spec((1, D_MODEL)), _const_spec((D_MODEL, D_MODEL)),
                  _const_spec((D_MODEL, D_MODEL))],
        out_specs=(blk_heads, blk_heads, blk_t, blk),
        compiler_params=_params("arbitrary"),
        name="mem_kv",
    )(mem, g, wk, wv)


def _run_side_by_side(stage_generators):
    live = list(stage_generators)
    while live:
        live = [g for g in live if next(g, StopIteration) is not StopIteration]


def _ffn_kernel(*refs, pre, final, sub):
    refs = list(refs)
    x_ref = refs.pop(0)
    if pre:
        o_ref_in, wco_ref = refs.pop(0), refs.pop(0)
    g_ref, w1_ref, w3_ref, w2_ref = refs[:4]
    refs = refs[4:]
    if final:
        gf_ref = refs.pop(0)
    out_ref = refs.pop(0)

    def sub_tile(r0):
        rows = slice(r0, r0 + sub)
        x = x_ref[rows, :]
        if pre:
            x = x + _dot(o_ref_in[rows, :], wco_ref[...])
        n = _rmsnorm(x, g_ref[...]).astype(BF16)
        yield
        a = _dot(n, w1_ref[...])
        yield
        b = _dot(n, w3_ref[...])
        yield
        mid = (jax.nn.silu(a) * b).astype(BF16)
        y = x + 0.5 * _dot(mid, w2_ref[...])
        if final:
            y = _rmsnorm(y, gf_ref[...])
        out_ref[rows, :] = y

    _run_side_by_side([sub_tile(r0) for r0 in range(0, x_ref.shape[0], sub)])


def _ffn(x, g, w1, w3, w2, *, o=None, wco=None, g_final=None, name):
    rows = x.shape[0]
    tm = min(FFN_ROWS, rows)
    assert rows % tm == 0
    pre, final = o is not None, g_final is not None
    row_spec = pl.BlockSpec((tm, D_MODEL), lambda i: (i, 0))
    args, specs = [x], [row_spec]
    if pre:
        args += [o, wco]
        specs += [row_spec, _const_spec((D_MODEL, D_MODEL))]
    args += [g, w1, w3, w2]
    specs += [_const_spec((1, D_MODEL)), _const_spec((D_MODEL, D_FF)), _const_spec((D_MODEL, D_FF)),
              _const_spec((D_FF, D_MODEL))]
    if final:
        args.append(g_final)
        specs.append(_const_spec((1, D_MODEL)))
    return pl.pallas_call(
        functools.partial(_ffn_kernel, pre=pre, final=final, sub=min(FFN_SUB_ROWS, tm)),
        out_shape=jax.ShapeDtypeStruct((rows, D_MODEL), F32),
        grid=(rows // tm,),
        in_specs=specs,
        out_specs=row_spec,
        compiler_params=_params("arbitrary"),
        name=name,
    )(*args)


def _in_proj(h, g_mix, w_in, b_gate):
    n = _rmsnorm(h, g_mix).astype(BF16)
    z = _dot(n, w_in)
    xa = z[:, :D_POOL]
    uv = jax.nn.gelu(z[:, D_POOL:D_POOL + 2 * D_SGU])
    gate = jax.nn.sigmoid(z[:, D_POOL + 2 * D_SGU:] + b_gate)
    return xa, uv[:, :D_SGU], uv[:, D_SGU:], gate[:, :D_MODEL], gate[:, D_MODEL:]


def _pool_project(pooled, xa, wpool_ref, pool_scale):
    mixed = []
    for g in range(POOL_GROUPS):
        sl = slice(g * POOL_GW, (g + 1) * POOL_GW)
        d = (pooled[g] - xa[:, sl]).astype(BF16)
        mixed.append(_dot(d, wpool_ref[g]))
    return jnp.concatenate(mixed, axis=-1) * pool_scale


def _merge(h, a, us, g_a, g_b, wpa_ref, wpb_ref, wo_ref):
    merged = g_a * _dot(a.astype(BF16), wpa_ref[...]) + g_b * _dot(us.astype(BF16), wpb_ref[...])
    return h + _dot(merged.astype(BF16), wo_ref[...])


def _softmax_pv(sc, v):
    sc = sc * (MEM_HEAD_DIM ** -0.5)
    e = jnp.exp(sc - jnp.max(sc, axis=-1, keepdims=True))
    p = e / jnp.sum(e, axis=-1, keepdims=True)
    return _dot(p.astype(BF16), v)


def _window_sums(halo, xa_g, window):
    s = jnp.concatenate([halo, xa_g], axis=0)
    step = 1
    while step < window:
        s = s + pltpu.roll(s, step, axis=0)
        step *= 2
    return s[POOL_HALO:]


def _mixer_prompt_kernel(h_ref, kt_ref, v_ref, gmix_ref, win_ref, bgate_ref, wpool_ref, pscale_ref,
                         gsgu_ref, ws_ref, bst_ref, wpa_ref, wpb_ref, wo_ref, gca_ref, wq_ref, wco_ref,
                         out_ref, pool_ref, carry_ref, *, tm, sub):
    j = pl.program_id(1)

    @pl.when(j == 0)
    def _():
        carry_ref[...] = jnp.zeros((POOL_HALO, D_POOL), F32)

    blk_r = lax.broadcasted_iota(jnp.int32, (SGU_CHUNK, SGU_CHUNK), 0) // CHUNK
    blk_c = lax.broadcasted_iota(jnp.int32, (SGU_CHUNK, SGU_CHUNK), 1) // CHUNK
    wm = [jnp.where(blk_r >= blk_c, ws_ref[hd], 0.0).astype(BF16) for hd in range(SGU_HEADS)]
    n_chunks = sub // SGU_CHUNK

    n_sub = tm // sub
    halos = [carry_ref[...]] + [None] * n_sub

    def sub_tile(i):
        r0 = i * sub
        h = h_ref[0, r0:r0 + sub, :]
        xa, u, v, g_a, g_b = _in_proj(h, gmix_ref[...], win_ref[...], bgate_ref[...])
        halos[i + 1] = xa[sub - POOL_HALO:, :]
        yield

        pos = j * tm + r0 + lax.broadcasted_iota(jnp.int32, (sub, 1), 0)
        pooled = []
        for g, w in enumerate(POOL_WINDOWS):
            sl = slice(g * POOL_GW, (g + 1) * POOL_GW)
            cnt = jnp.minimum(w, pos + 1).astype(F32)
            pooled.append(_window_sums(halos[i][:, sl], xa[:, sl], w) / cnt)
        a = _pool_project(pooled, xa, wpool_ref, pscale_ref[...])
        yield

        vn = _layernorm(v, gsgu_ref[...]).astype(BF16)
        s_heads = []
        for hd in range(SGU_HEADS):
            cols = slice(hd * SGU_HW, (hd + 1) * SGU_HW)
            vcat = jnp.concatenate([vn[c * SGU_CHUNK:(c + 1) * SGU_CHUNK, cols] for c in range(n_chunks)],
                                   axis=-1)
            sh = _dot(wm[hd], vcat) + bst_ref[:, hd:hd + 1]
            s_heads.append(jnp.concatenate([sh[:, c * SGU_HW:(c + 1) * SGU_HW] for c in range(n_chunks)],
                                           axis=0))
        s = jnp.concatenate(s_heads, axis=-1)
        yield

        h = _merge(h, a, u * s, g_a, g_b, wpa_ref, wpb_ref, wo_ref)
        yield
        q = _dot(_rmsnorm(h, gca_ref[...]).astype(BF16), wq_ref[...]).astype(BF16)
        yield
        outs = []
        for hd in range(MEM_HEADS):
            sl = slice(hd * MEM_HEAD_DIM, (hd + 1) * MEM_HEAD_DIM)
            outs.append(_softmax_pv(_dot(q[:, sl], kt_ref[0, sl, :]), v_ref[0, :, sl]))
            yield
        o = jnp.concatenate(outs, axis=-1)
        out_ref[0, r0:r0 + sub, :] = h + _dot(o.astype(BF16), wco_ref[...])

    _run_side_by_side([sub_tile(i) for i in range(n_sub)])

    carry_ref[...] = halos[n_sub]

    @pl.when(j == pl.num_programs(1) - 1)
    def _():
        pool_ref[0] = halos[n_sub]


def _mixer_prompt(h, kb, vb, w):
    nb, seq, _ = h.shape
    tm, sub = MIX_ROWS, MIX_SUB_ROWS
    assert seq % tm == 0 and tm % sub == 0 and sub % SGU_CHUNK == 0
    row_spec = pl.BlockSpec((1, tm, D_MODEL), lambda b, j: (b, j, 0))
    kt_spec = pl.BlockSpec((1, D_MODEL, N_MEM), lambda b, j: (b, 0, 0))
    v_spec = pl.BlockSpec((1, N_MEM, D_MODEL), lambda b, j: (b, 0, 0))
    consts = [w["g_mix"], w["w_in"], w["b_gate"], w["w_pool"], w["pool_scale"], w["g_sgu"], w["w_s"],
              w["b_s_t"], w["w_pa"], w["w_pb"], w["w_o"], w["g_ca"], w["w_q"], w["w_co"]]
    return pl.pallas_call(
        functools.partial(_mixer_prompt_kernel, tm=tm, sub=sub),
        out_shape=(jax.ShapeDtypeStruct((nb, seq, D_MODEL), F32),
                   jax.ShapeDtypeStruct((nb, POOL_HALO, D_POOL), F32)),
        grid=(nb, seq // tm),
        in_specs=[row_spec, kt_spec, v_spec] + [_const_spec(c.shape) for c in consts],
        out_specs=(row_spec, pl.BlockSpec((1, POOL_HALO, D_POOL), lambda b, j: (b, 0, 0))),
        scratch_shapes=[pltpu.VMEM((POOL_HALO, D_POOL), F32)],
        compiler_params=_params("arbitrary", "arbitrary"),
        name="mixer_prompt",
    )(h, kb, vb, *consts)


def _mixer_sample_kernel(h_ref, state_ref, gmix_ref, win_ref, bgate_ref, wpool_ref, pscale_ref,
                         gsgu_ref, wsx_ref, bsx_ref, wpa_ref, wpb_ref, wo_ref, gca_ref, wq_ref,
                         out_ref, q_ref, pool_ref, vn_ref, *, nb, t, past_len):
    rows = t * nb
    halo = POOL_STATE * nb
    h = h_ref[...]
    xa, u, v, g_a, g_b = _in_proj(h, gmix_ref[...], win_ref[...], bgate_ref[...])

    cat = jnp.concatenate([state_ref[...], xa], axis=0)
    pooled = []
    for g, w in enumerate(POOL_WINDOWS):
        sl = slice(g * POOL_GW, (g + 1) * POOL_GW)
        acc = xa[:, sl]
        for back in range(1, w):
            acc = acc + cat[halo - back * nb:halo - back * nb + rows, sl]
        pos = past_len + lax.broadcasted_iota(jnp.int32, (rows, 1), 0) // nb
        cnt = jnp.minimum(w, pos + 1).astype(F32)
        pooled.append(acc / cnt)
    a = _pool_project(pooled, xa, wpool_ref, pscale_ref[...])
    pool_ref[...] = cat[rows:rows + halo, :]

    vn = _layernorm(v, gsgu_ref[...])
    vn_ref[...] = vn
    s_rows = []
    for p in range(t):
        acc = jnp.broadcast_to(bsx_ref[p:p + 1, :], (nb, D_SGU))
        for qq in range(t):
            if p // CHUNK >= qq // CHUNK:
                acc = acc + wsx_ref[p, qq:qq + 1, :] * vn[qq * nb:(qq + 1) * nb, :]
        s_rows.append(acc)
    s = jnp.concatenate(s_rows, axis=0)

    h = _merge(h, a, u * s, g_a, g_b, wpa_ref, wpb_ref, wo_ref)
    out_ref[...] = h
    q_ref[...] = _dot(_rmsnorm(h, gca_ref[...]).astype(BF16), wq_ref[...]).astype(BF16)


def _mixer_sample(h, state, w, *, nb, t, past_len):
    rows = nb * t
    consts = [w["g_mix"], w["w_in"], w["b_gate"], w["w_pool"], w["pool_scale"], w["g_sgu"], w["w_s_x"],
              w["b_s_x"], w["w_pa"], w["w_pb"], w["w_o"], w["g_ca"], w["w_q"]]
    args = [h, state] + consts
    return pl.pallas_call(
        functools.partial(_mixer_sample_kernel, nb=nb, t=t, past_len=past_len),
        out_shape=(jax.ShapeDtypeStruct((rows, D_MODEL), F32),
                   jax.ShapeDtypeStruct((rows, D_MODEL), BF16),
                   jax.ShapeDtypeStruct((POOL_STATE * nb, D_POOL), F32),
                   jax.ShapeDtypeStruct((rows, D_SGU), F32)),
        grid=(1,),
        in_specs=[_const_spec(a.shape) for a in args],
        out_specs=(_const_spec((rows, D_MODEL)), _const_spec((rows, D_MODEL)),
                   _const_spec((POOL_STATE * nb, D_POOL)), _const_spec((rows, D_SGU))),
        compiler_params=_params("arbitrary"),
        name="mixer_sample",
    )(*args)


def _attn_sample_kernel(q_ref, k_ref, v_ref, o_ref, *, streams):
    def stream(s):
        q = q_ref[s]
        outs = []
        for hd in range(MEM_HEADS):
            sl = slice(hd * MEM_HEAD_DIM, (hd + 1) * MEM_HEAD_DIM)
            sc = lax.dot_general(q[:, sl], k_ref[s, :, sl], (((1,), (1,)), ((), ())),
                                 preferred_element_type=F32)
            outs.append(_softmax_pv(sc, v_ref[s, :, sl]))
            yield
        o_ref[s] = jnp.concatenate(outs, axis=-1).astype(BF16)

    _run_side_by_side([stream(s) for s in range(streams)])


def _attn_sample(q, k, v):
    nb, t, _ = q.shape
    streams = ATTN_SAMPLE_STREAMS
    assert nb % streams == 0
    q_spec = pl.BlockSpec((streams, t, D_MODEL), lambda b: (b, 0, 0))
    kv_spec = pl.BlockSpec((streams, N_MEM, D_MODEL), lambda b: (b, 0, 0))
    return pl.pallas_call(
        functools.partial(_attn_sample_kernel, streams=streams),
        out_shape=jax.ShapeDtypeStruct((nb, t, D_MODEL), BF16),
        grid=(nb // streams,),
        in_specs=[q_spec, kv_spec, kv_spec],
        out_specs=q_spec,
        compiler_params=_params("arbitrary"),
        name="attn_sample",
    )(q, k, v)


def kernel(x_prompt, x_sample, state_pool, cache_mem_k, cache_mem_v, mem_prompt, g_ff1, w1a, w3a, w2a,
           g_mix, w_in, b_gate, w_pool, pool_scale, g_sgu, w_s, b_s, w_pa, w_pb, w_o, g_mem, w_mk, w_mv,
           g_ca, w_q, w_co, g_ff2, w1b, w3b, w2b, g_final):
    nb, seq, _ = x_prompt.shape
    nbs, t, _ = x_sample.shape
    depth = g_ff1.shape[0]
    assert depth == 1
    l = 0
    row = lambda a: a.reshape(1, -1)
    bf = lambda a: a.astype(BF16)

    mat = lambda a: a[l].reshape(-1, a.shape[-1])
    (w1a_b, w3a_b, w2a_b, w1b_b, w3b_b, w2b_b, w_in_b, w_pa_b, w_pb_b, w_o_b, w_q_b, w_co_b, w_mk_b, w_mv_b,
     w_pool_b) = _cast_bf16([mat(a) for a in (w1a, w3a, w2a, w1b, w3b, w2b, w_in, w_pa, w_pb, w_o, w_q, w_co,
                                              w_mk, w_mv, w_pool)])
    w = dict(
        g_mix=row(g_mix[l]), w_in=w_in_b, b_gate=row(b_gate[l]),
        w_pool=w_pool_b.reshape(POOL_GROUPS, POOL_GW, POOL_GW),
        pool_scale=row(pool_scale[l]), g_sgu=row(g_sgu[l]), w_s=w_s[l], b_s_t=b_s[l].T,
        w_pa=w_pa_b, w_pb=w_pb_b, w_o=w_o_b, g_ca=row(g_ca[l]), w_q=w_q_b, w_co=w_co_b,
        w_s_x=jnp.repeat(jnp.transpose(w_s[l][:, :t, :t], (1, 2, 0)), SGU_HW, axis=-1),
        b_s_x=jnp.repeat(b_s[l][:, :t].T, SGU_HW, axis=-1),
    )

    mk, mv, mk_t, mv_b = _mem_kv(mem_prompt, row(g_mem[l]), w_mk_b, w_mv_b)
    hp = _ffn(x_prompt.reshape(nb * seq, D_MODEL), row(g_ff1[l]), w1a_b, w3a_b, w2a_b, name="ffn1_prompt")
    hp, pool_p = _mixer_prompt(hp.reshape(nb, seq, D_MODEL), mk_t, mv_b, w)
    y_prompt = _ffn(hp.reshape(nb * seq, D_MODEL), row(g_ff2[l]), w1b_b, w3b_b, w2b_b,
                    g_final=row(g_final), name="ffn2_prompt").reshape(nb, seq, D_MODEL)

    xs = jnp.transpose(x_sample, (1, 0, 2)).reshape(t * nbs, D_MODEL)
    st = jnp.transpose(state_pool[l], (1, 0, 2)).reshape(POOL_STATE * nbs, D_POOL)
    hs = _ffn(xs, row(g_ff1[l]), w1a_b, w3a_b, w2a_b, name="ffn1_sample")
    hs, qs, pool_s, vn_s = _mixer_sample(hs, st, w, nb=nbs, t=t, past_len=PAST_LEN)
    to_stream_major = lambda a, n: jnp.transpose(a.reshape(n, nbs, a.shape[-1]), (1, 0, 2))
    hs = to_stream_major(hs, t)
    qs = to_stream_major(qs, t)
    os_ = _attn_sample(qs, bf(cache_mem_k[l].reshape(nbs, N_MEM, D_MODEL)),
                       bf(cache_mem_v[l].reshape(nbs, N_MEM, D_MODEL)))
    y_sample = _ffn(hs.reshape(nbs * t, D_MODEL), row(g_ff2[l]), w1b_b, w3b_b, w2b_b,
                    o=os_.reshape(nbs * t, D_MODEL), wco=w["w_co"], g_final=row(g_final),
                    name="ffn2_sample").reshape(nbs, t, D_MODEL)

    pool_prompt = pool_p[:, POOL_HALO - POOL_STATE:, :][None]
    pool_sample = to_stream_major(pool_s, POOL_STATE)[None]
    sgu_v_sample = to_stream_major(vn_s, t)[None]
    return (y_prompt, y_sample, pool_prompt, pool_sample, sgu_v_sample, mk, mv)
```

```python
import functools

import jax
import jax.numpy as jnp
from jax import lax
from jax.experimental import pallas as pl
from jax.experimental.pallas import tpu as pltpu

D_MODEL = 1024
PAST_LEN = 1024
CHUNK = 64
N_MEM = 256
MEM_HEADS = 4
MEM_HEAD_DIM = D_MODEL // MEM_HEADS
D_POOL = D_MODEL // 2
POOL_WINDOWS = (2, 4, 8, 16)
POOL_GROUPS = len(POOL_WINDOWS)
POOL_GW = D_POOL // POOL_GROUPS
POOL_STATE = max(POOL_WINDOWS) - 1
D_SGU = D_MODEL // 2
SGU_HEADS = 4
SGU_HW = D_SGU // SGU_HEADS
SGU_CHUNK = 128
D_FF = 2816
D_IN = D_POOL + 2 * D_SGU + 2 * D_MODEL
EPS = 1e-6

V7X_VMEM_LIMIT_BYTES = 56 * 1024 * 1024
F32_SUBLANES = 8
BF16_SUBLANES = 16
LANES = 128
CAST_STEPS = 16
POOL_HALO = 2 * F32_SUBLANES
assert POOL_HALO >= POOL_STATE + 1

FFN_ROWS = 1024
FFN_SUB_ROWS = 256
FFN_SIDE_BY_SIDE = 2
MIX_ROWS = 512
MIX_SUB_ROWS = 256
ATTN_SAMPLE_STREAMS = 4

BF16 = jnp.bfloat16
F32 = jnp.float32


def _dot(a, b):
    return jnp.dot(a, b, preferred_element_type=F32)


def _rmsnorm(x, g):
    y = x * lax.rsqrt(jnp.mean(x * x, axis=-1, keepdims=True) + EPS)
    return y * g


def _layernorm(x, g):
    mu = jnp.mean(x, axis=-1, keepdims=True)
    xc = x - mu
    y = xc * lax.rsqrt(jnp.mean(xc * xc, axis=-1, keepdims=True) + EPS)
    return y * g


def _const_spec(shape):
    nd = len(shape)
    return pl.BlockSpec(shape, lambda *_: (0,) * nd, pipeline_mode=pl.Buffered(1))


def _params(*sem):
    return pltpu.CompilerParams(dimension_semantics=sem, vmem_limit_bytes=V7X_VMEM_LIMIT_BYTES)


def _cast_kernel(*refs):
    n = len(refs) // 2
    for src, dst in zip(refs[:n], refs[n:]):
        dst[...] = src[...].astype(BF16)


def _cast_bf16(arrays):
    for a in arrays:
        assert a.ndim == 2 and a.shape[0] % (CAST_STEPS * BF16_SUBLANES) == 0 and a.shape[1] % LANES == 0
    specs = [pl.BlockSpec((a.shape[0] // CAST_STEPS, a.shape[1]), lambda i: (i, 0)) for a in arrays]
    return pl.pallas_call(
        _cast_kernel,
        out_shape=tuple(jax.ShapeDtypeStruct(a.shape, BF16) for a in arrays),
        grid=(CAST_STEPS,),
        in_specs=specs,
        out_specs=tuple(specs),
        compiler_params=_params("arbitrary"),
        name="cast_weights",
    )(*arrays)


def _mem_kv_kernel(mem_ref, g_ref, wk_ref, wv_ref, k_ref, v_ref, kt_ref, vb_ref):
    mn = _rmsnorm(mem_ref[0], g_ref[...]).astype(BF16)
    k = _dot(mn, wk_ref[...])
    v = _dot(mn, wv_ref[...])
    for hd in range(MEM_HEADS):
        sl = slice(hd * MEM_HEAD_DIM, (hd + 1) * MEM_HEAD_DIM)
        k_ref[0, 0, :, hd, :] = k[:, sl]
        v_ref[0, 0, :, hd, :] = v[:, sl]
    kt_ref[0] = k.T.astype(BF16)
    vb_ref[0] = v.astype(BF16)


def _mem_kv(mem, g, wk, wv):
    nb = mem.shape[0]
    blk = pl.BlockSpec((1, N_MEM, D_MODEL), lambda b: (b, 0, 0))
    blk_t = pl.BlockSpec((1, D_MODEL, N_MEM), lambda b: (b, 0, 0))
    blk_heads = pl.BlockSpec((1, 1, N_MEM, MEM_HEADS, MEM_HEAD_DIM), lambda b: (0, b, 0, 0, 0))
    return pl.pallas_call(
        _mem_kv_kernel,
        out_shape=(jax.ShapeDtypeStruct((1, nb, N_MEM, MEM_HEADS, MEM_HEAD_DIM), F32),) * 2
        + (jax.ShapeDtypeStruct((nb, D_MODEL, N_MEM), BF16), jax.ShapeDtypeStruct((nb, N_MEM, D_MODEL), BF16)),
        grid=(nb,),
        in_specs=[blk, _const_spec((1, D_MODEL)), _const_spec((D_MODEL, D_MODEL)),
                  _const_spec((D_MODEL, D_MODEL))],
        out_specs=(blk_heads, blk_heads, blk_t, blk),
        compiler_params=_params("arbitrary"),
        name="mem_kv",
    )(mem, g, wk, wv)


def _run_side_by_side(stage_generators):
    live = list(stage_generators)
    while live:
        live = [g for g in live if next(g, StopIteration) is not StopIteration]


def _ffn_kernel(*refs, pre, final, sub):
    refs = list(refs)
    x_ref = refs.pop(0)
    if pre:
        o_ref_in, wco_ref = refs.pop(0), refs.pop(0)
    g_ref, w1_ref, w3_ref, w2_ref = refs[:4]
    refs = refs[4:]
    if final:
        gf_ref = refs.pop(0)
    out_ref = refs.pop(0)

    def sub_tile(r0):
        rows = slice(r0, r0 + sub)
        x = x_ref[rows, :]
        if pre:
            x = x + _dot(o_ref_in[rows, :], wco_ref[...])
        n = _rmsnorm(x, g_ref[...]).astype(BF16)
        yield
        a = _dot(n, w1_ref[...])
        yield
        b = _dot(n, w3_ref[...])
        yield
        mid = (jax.nn.silu(a) * b).astype(BF16)
        y = x + 0.5 * _dot(mid, w2_ref[...])
        if final:
            y = _rmsnorm(y, gf_ref[...])
        out_ref[rows, :] = y

    starts = list(range(0, x_ref.shape[0], sub))
    for i in range(0, len(starts), FFN_SIDE_BY_SIDE):
        _run_side_by_side([sub_tile(r0) for r0 in starts[i:i + FFN_SIDE_BY_SIDE]])


def _ffn(x, g, w1, w3, w2, *, o=None, wco=None, g_final=None, name):
    rows = x.shape[0]
    tm = min(FFN_ROWS, rows)
    assert rows % tm == 0
    pre, final = o is not None, g_final is not None
    row_spec = pl.BlockSpec((tm, D_MODEL), lambda i: (i, 0))
    args, specs = [x], [row_spec]
    if pre:
        args += [o, wco]
        specs += [row_spec, _const_spec((D_MODEL, D_MODEL))]
    args += [g, w1, w3, w2]
    specs += [_const_spec((1, D_MODEL)), _const_spec((D_MODEL, D_FF)), _const_spec((D_MODEL, D_FF)),
              _const_spec((D_FF, D_MODEL))]
    if final:
        args.append(g_final)
        specs.append(_const_spec((1, D_MODEL)))
    return pl.pallas_call(
        functools.partial(_ffn_kernel, pre=pre, final=final, sub=min(FFN_SUB_ROWS, tm)),
        out_shape=jax.ShapeDtypeStruct((rows, D_MODEL), F32),
        grid=(rows // tm,),
        in_specs=specs,
        out_specs=row_spec,
        compiler_params=_params("arbitrary"),
        name=name,
    )(*args)


def _in_proj(h, g_mix, w_in, b_gate):
    n = _rmsnorm(h, g_mix).astype(BF16)
    z = _dot(n, w_in)
    xa = z[:, :D_POOL]
    uv = jax.nn.gelu(z[:, D_POOL:D_POOL + 2 * D_SGU])
    gate = jax.nn.sigmoid(z[:, D_POOL + 2 * D_SGU:] + b_gate)
    return xa, uv[:, :D_SGU], uv[:, D_SGU:], gate[:, :D_MODEL], gate[:, D_MODEL:]


def _pool_project(pooled, xa, wpool_ref, pool_scale):
    mixed = []
    for g in range(POOL_GROUPS):
        sl = slice(g * POOL_GW, (g + 1) * POOL_GW)
        d = (pooled[g] - xa[:, sl]).astype(BF16)
        mixed.append(_dot(d, wpool_ref[g]))
    return jnp.concatenate(mixed, axis=-1) * pool_scale


def _merge(h, a, us, g_a, g_b, wpa_ref, wpb_ref, wo_ref):
    merged = g_a * _dot(a.astype(BF16), wpa_ref[...]) + g_b * _dot(us.astype(BF16), wpb_ref[...])
    return h + _dot(merged.astype(BF16), wo_ref[...])


def _softmax_pv(sc, v):
    sc = sc * (MEM_HEAD_DIM ** -0.5)
    e = jnp.exp(sc - jnp.max(sc, axis=-1, keepdims=True))
    p = e / jnp.sum(e, axis=-1, keepdims=True)
    return _dot(p.astype(BF16), v)


def _window_sums(halo, xa_g, window):
    s = jnp.concatenate([halo, xa_g], axis=0)
    step = 1
    while step < window:
        s = s + pltpu.roll(s, step, axis=0)
        step *= 2
    return s[POOL_HALO:]


def _mixer_prompt_kernel(h_ref, kt_ref, v_ref, gmix_ref, win_ref, bgate_ref, wpool_ref, pscale_ref,
                         gsgu_ref, ws_ref, bst_ref, wpa_ref, wpb_ref, wo_ref, gca_ref, wq_ref, wco_ref,
                         out_ref, pool_ref, carry_ref, *, tm, sub):
    j = pl.program_id(1)

    @pl.when(j == 0)
    def _():
        carry_ref[...] = jnp.zeros((POOL_HALO, D_POOL), F32)

    blk_r = lax.broadcasted_iota(jnp.int32, (SGU_CHUNK, SGU_CHUNK), 0) // CHUNK
    blk_c = lax.broadcasted_iota(jnp.int32, (SGU_CHUNK, SGU_CHUNK), 1) // CHUNK
    wm = [jnp.where(blk_r >= blk_c, ws_ref[hd], 0.0).astype(BF16) for hd in range(SGU_HEADS)]
    n_chunks = sub // SGU_CHUNK

    n_sub = tm // sub
    halos = [carry_ref[...]] + [None] * n_sub

    def sub_tile(i):
        r0 = i * sub
        h = h_ref[0, r0:r0 + sub, :]
        xa, u, v, g_a, g_b = _in_proj(h, gmix_ref[...], win_ref[...], bgate_ref[...])
        halos[i + 1] = xa[sub - POOL_HALO:, :]
        yield

        pos = j * tm + r0 + lax.broadcasted_iota(jnp.int32, (sub, 1), 0)
        pooled = []
        for g, w in enumerate(POOL_WINDOWS):
            sl = slice(g * POOL_GW, (g + 1) * POOL_GW)
            cnt = jnp.minimum(w, pos + 1).astype(F32)
            pooled.append(_window_sums(halos[i][:, sl], xa[:, sl], w) / cnt)
        a = _pool_project(pooled, xa, wpool_ref, pscale_ref[...])
        yield

        vn = _layernorm(v, gsgu_ref[...]).astype(BF16)
        s_heads = []
        for hd in range(SGU_HEADS):
            cols = slice(hd * SGU_HW, (hd + 1) * SGU_HW)
            vcat = jnp.concatenate([vn[c * SGU_CHUNK:(c + 1) * SGU_CHUNK, cols] for c in range(n_chunks)],
                                   axis=-1)
            sh = _dot(wm[hd], vcat) + bst_ref[:, hd:hd + 1]
            s_heads.append(jnp.concatenate([sh[:, c * SGU_HW:(c + 1) * SGU_HW] for c in range(n_chunks)],
                                           axis=0))
        s = jnp.concatenate(s_heads, axis=-1)
        yield

        h = _merge(h, a, u * s, g_a, g_b, wpa_ref, wpb_ref, wo_ref)
        yield
        q = _dot(_rmsnorm(h, gca_ref[...]).astype(BF16), wq_ref[...]).astype(BF16)
        yield
        outs = []
        for hd in range(MEM_HEADS):
            sl = slice(hd * MEM_HEAD_DIM, (hd + 1) * MEM_HEAD_DIM)
            outs.append(_softmax_pv(_dot(q[:, sl], kt_ref[0, sl, :]), v_ref[0, :, sl]))
            yield
        o = jnp.concatenate(outs, axis=-1)
        out_ref[0, r0:r0 + sub, :] = h + _dot(o.astype(BF16), wco_ref[...])

    _run_side_by_side([sub_tile(i) for i in range(n_sub)])

    carry_ref[...] = halos[n_sub]

    @pl.when(j == pl.num_programs(1) - 1)
    def _():
        pool_ref[0] = halos[n_sub]


def _mixer_prompt(h, kb, vb, w):
    nb, seq, _ = h.shape
    tm, sub = MIX_ROWS, MIX_SUB_ROWS
    assert seq % tm == 0 and tm % sub == 0 and sub % SGU_CHUNK == 0
    row_spec = pl.BlockSpec((1, tm, D_MODEL), lambda b, j: (b, j, 0))
    kt_spec = pl.BlockSpec((1, D_MODEL, N_MEM), lambda b, j: (b, 0, 0))
    v_spec = pl.BlockSpec((1, N_MEM, D_MODEL), lambda b, j: (b, 0, 0))
    consts = [w["g_mix"], w["w_in"], w["b_gate"], w["w_pool"], w["pool_scale"], w["g_sgu"], w["w_s"],
              w["b_s_t"], w["w_pa"], w["w_pb"], w["w_o"], w["g_ca"], w["w_q"], w["w_co"]]
    return pl.pallas_call(
        functools.partial(_mixer_prompt_kernel, tm=tm, sub=sub),
        out_shape=(jax.ShapeDtypeStruct((nb, seq, D_MODEL), F32),
                   jax.ShapeDtypeStruct((nb, POOL_HALO, D_POOL), F32)),
        grid=(nb, seq // tm),
        in_specs=[row_spec, kt_spec, v_spec] + [_const_spec(c.shape) for c in consts],
        out_specs=(row_spec, pl.BlockSpec((1, POOL_HALO, D_POOL), lambda b, j: (b, 0, 0))),
        scratch_shapes=[pltpu.VMEM((POOL_HALO, D_POOL), F32)],
        compiler_params=_params("arbitrary", "arbitrary"),
        name="mixer_prompt",
    )(h, kb, vb, *consts)


def _mixer_sample_kernel(h_ref, state_ref, gmix_ref, win_ref, bgate_ref, wpool_ref, pscale_ref,
                         gsgu_ref, wsx_ref, bsx_ref, wpa_ref, wpb_ref, wo_ref, gca_ref, wq_ref,
                         out_ref, q_ref, pool_ref, vn_ref, *, nb, t, past_len):
    rows = t * nb
    halo = POOL_STATE * nb
    h = h_ref[...]
    xa, u, v, g_a, g_b = _in_proj(h, gmix_ref[...], win_ref[...], bgate_ref[...])

    cat = jnp.concatenate([state_ref[...], xa], axis=0)
    pooled = []
    for g, w in enumerate(POOL_WINDOWS):
        sl = slice(g * POOL_GW, (g + 1) * POOL_GW)
        acc = xa[:, sl]
        for back in range(1, w):
            acc = acc + cat[halo - back * nb:halo - back * nb + rows, sl]
        pos = past_len + lax.broadcasted_iota(jnp.int32, (rows, 1), 0) // nb
        cnt = jnp.minimum(w, pos + 1).astype(F32)
        pooled.append(acc / cnt)
    a = _pool_project(pooled, xa, wpool_ref, pscale_ref[...])
    pool_ref[...] = cat[rows:rows + halo, :]

    vn = _layernorm(v, gsgu_ref[...])
    vn_ref[...] = vn
    s_rows = []
    for p in range(t):
        acc = jnp.broadcast_to(bsx_ref[p:p + 1, :], (nb, D_SGU))
        for qq in range(t):
            if p // CHUNK >= qq // CHUNK:
                acc = acc + wsx_ref[p, qq:qq + 1, :] * vn[qq * nb:(qq + 1) * nb, :]
        s_rows.append(acc)
    s = jnp.concatenate(s_rows, axis=0)

    h = _merge(h, a, u * s, g_a, g_b, wpa_ref, wpb_ref, wo_ref)
    out_ref[...] = h
    q_ref[...] = _dot(_rmsnorm(h, gca_ref[...]).astype(BF16), wq_ref[...]).astype(BF16)


def _mixer_sample(h, state, w, *, nb, t, past_len):
    rows = nb * t
    consts = [w["g_mix"], w["w_in"], w["b_gate"], w["w_pool"], w["pool_scale"], w["g_sgu"], w["w_s_x"],
              w["b_s_x"], w["w_pa"], w["w_pb"], w["w_o"], w["g_ca"], w["w_q"]]
    args = [h, state] + consts
    return pl.pallas_call(
        functools.partial(_mixer_sample_kernel, nb=nb, t=t, past_len=past_len),
        out_shape=(jax.ShapeDtypeStruct((rows, D_MODEL), F32),
                   jax.ShapeDtypeStruct((rows, D_MODEL), BF16),
                   jax.ShapeDtypeStruct((POOL_STATE * nb, D_POOL), F32),
                   jax.ShapeDtypeStruct((rows, D_SGU), F32)),
        grid=(1,),
        in_specs=[_const_spec(a.shape) for a in args],
        out_specs=(_const_spec((rows, D_MODEL)), _const_spec((rows, D_MODEL)),
                   _const_spec((POOL_STATE * nb, D_POOL)), _const_spec((rows, D_SGU))),
        compiler_params=_params("arbitrary"),
        name="mixer_sample",
    )(*args)


def _attn_sample_kernel(q_ref, k_ref, v_ref, o_ref, *, streams):
    t = q_ref.shape[1]
    n_cols = N_MEM * MEM_HEADS
    row_head = lax.broadcasted_iota(jnp.int32, (MEM_HEADS * t, n_cols), 0) // t
    col_head = lax.broadcasted_iota(jnp.int32, (MEM_HEADS * t, n_cols), 1) % MEM_HEADS
    own_head = row_head == col_head

    def stream(s):
        q = q_ref[s]
        q_rows = jnp.concatenate([q[:, hd * MEM_HEAD_DIM:(hd + 1) * MEM_HEAD_DIM] for hd in range(MEM_HEADS)],
                                 axis=0)
        k = k_ref[0, s].reshape(n_cols, MEM_HEAD_DIM).astype(BF16)
        v = v_ref[0, s].reshape(n_cols, MEM_HEAD_DIM).astype(BF16)
        yield
        sc = lax.dot_general(q_rows, k, (((1,), (1,)), ((), ())), preferred_element_type=F32)
        sc = jnp.where(own_head, sc * (MEM_HEAD_DIM ** -0.5), -jnp.inf)
        yield
        e = jnp.exp(sc - jnp.max(sc, axis=-1, keepdims=True))
        p = e / jnp.sum(e, axis=-1, keepdims=True)
        yield
        o = _dot(p.astype(BF16), v)
        o_ref[s] = jnp.concatenate([o[hd * t:(hd + 1) * t, :] for hd in range(MEM_HEADS)],
                                   axis=-1).astype(BF16)

    _run_side_by_side([stream(s) for s in range(streams)])


def _attn_sample(q, k, v):
    nb, t, _ = q.shape
    streams = ATTN_SAMPLE_STREAMS
    assert nb % streams == 0
    q_spec = pl.BlockSpec((streams, t, D_MODEL), lambda b: (b, 0, 0))
    kv_spec = pl.BlockSpec((1, streams, N_MEM, MEM_HEADS, MEM_HEAD_DIM), lambda b: (0, b, 0, 0, 0))
    return pl.pallas_call(
        functools.partial(_attn_sample_kernel, streams=streams),
        out_shape=jax.ShapeDtypeStruct((nb, t, D_MODEL), BF16),
        grid=(nb // streams,),
        in_specs=[q_spec, kv_spec, kv_spec],
        out_specs=q_spec,
        compiler_params=_params("arbitrary"),
        name="attn_sample",
    )(q, k, v)


def kernel(x_prompt, x_sample, state_pool, cache_mem_k, cache_mem_v, mem_prompt, g_ff1, w1a, w3a, w2a,
           g_mix, w_in, b_gate, w_pool, pool_scale, g_sgu, w_s, b_s, w_pa, w_pb, w_o, g_mem, w_mk, w_mv,
           g_ca, w_q, w_co, g_ff2, w1b, w3b, w2b, g_final):
    nb, seq, _ = x_prompt.shape
    nbs, t, _ = x_sample.shape
    depth = g_ff1.shape[0]
    assert depth == 1
    l = 0
    row = lambda a: a.reshape(1, -1)
    mat =lambda a: a[l].reshape(-1, a.shape[-1])
    (w1a_b, w3a_b, w2a_b, w1b_b, w3b_b, w2b_b, w_in_b, w_pa_b, w_pb_b, w_o_b, w_q_b, w_co_b, w_mk_b, w_mv_b,
     w_pool_b) = _cast_bf16([mat(a) for a in (w1a, w3a, w2a, w1b, w3b, w2b, w_in, w_pa, w_pb, w_o, w_q, w_co,
                                              w_mk, w_mv, w_pool)])
    w = dict(
        g_mix=row(g_mix[l]), w_in=w_in_b, b_gate=row(b_gate[l]),
        w_pool=w_pool_b.reshape(POOL_GROUPS, POOL_GW, POOL_GW),
        pool_scale=row(pool_scale[l]), g_sgu=row(g_sgu[l]), w_s=w_s[l], b_s_t=b_s[l].T,
        w_pa=w_pa_b, w_pb=w_pb_b, w_o=w_o_b, g_ca=row(g_ca[l]), w_q=w_q_b, w_co=w_co_b,
        w_s_x=jnp.repeat(jnp.transpose(w_s[l][:, :t, :t], (1, 2, 0)), SGU_HW, axis=-1),
        b_s_x=jnp.repeat(b_s[l][:, :t].T, SGU_HW, axis=-1),
    )

    mk, mv, mk_t, mv_b = _mem_kv(mem_prompt, row(g_mem[l]), w_mk_b, w_mv_b)
    hp = _ffn(x_prompt.reshape(nb * seq, D_MODEL), row(g_ff1[l]), w1a_b, w3a_b, w2a_b, name="ffn1_prompt")
    hp, pool_p = _mixer_prompt(hp.reshape(nb, seq, D_MODEL), mk_t, mv_b, w)
    y_prompt = _ffn(hp.reshape(nb * seq, D_MODEL), row(g_ff2[l]), w1b_b, w3b_b, w2b_b,
                    g_final=row(g_final), name="ffn2_prompt").reshape(nb, seq, D_MODEL)

    xs = jnp.transpose(x_sample, (1, 0, 2)).reshape(t * nbs, D_MODEL)
    st = jnp.transpose(state_pool[l], (1, 0, 2)).reshape(POOL_STATE * nbs, D_POOL)
    hs = _ffn(xs, row(g_ff1[l]), w1a_b, w3a_b, w2a_b, name="ffn1_sample")
    hs, qs, pool_s, vn_s = _mixer_sample(hs, st, w, nb=nbs, t=t, past_len=PAST_LEN)
    to_stream_major = lambda a, n: jnp.transpose(a.reshape(n, nbs, a.shape[-1]), (1, 0, 2))
    hs = to_stream_major(hs, t)
    qs = to_stream_major(qs, t)
    os_ = _attn_sample(qs, cache_mem_k, cache_mem_v)
    y_sample = _ffn(hs.reshape(nbs * t, D_MODEL), row(g_ff2[l]), w1b_b, w3b_b, w2b_b,
                    o=os_.reshape(nbs * t, D_MODEL), wco=w["w_co"], g_final=row(g_final),
                    name="ffn2_sample").reshape(nbs, t, D_MODEL)

    pool_prompt = pool_p[:, POOL_HALO - POOL_STATE:, :][None]
    pool_sample = to_stream_major(pool_s, POOL_STATE)[None]
    sgu_v_sample = to_stream_major(vn_s, t)[None]
    return (y_prompt, y_sample, pool_prompt, pool_sample, sgu_v_sample, mk, mv)
```

```python
import functools

import jax
import jax.numpy as jnp
from jax import lax
from jax.experimental import pallas as pl
from jax.experimental.pallas import tpu as pltpu

D_MODEL = 1024
PAST_LEN = 1024
CHUNK = 64
N_MEM = 256
MEM_HEADS = 4
MEM_HEAD_DIM = D_MODEL // MEM_HEADS
D_POOL = D_MODEL // 2
POOL_WINDOWS = (2, 4, 8, 16)
POOL_GROUPS = len(POOL_WINDOWS)
POOL_GW = D_POOL // POOL_GROUPS
POOL_STATE = max(POOL_WINDOWS) - 1
D_SGU = D_MODEL // 2
SGU_HEADS = 4
SGU_HW = D_SGU // SGU_HEADS
SGU_CHUNK = 128
D_FF = 2816
D_IN = D_POOL + 2 * D_SGU + 2 * D_MODEL
EPS = 1e-6

V7X_VMEM_LIMIT_BYTES = 56 * 1024 * 1024
F32_SUBLANES = 8
BF16_SUBLANES = 16
LANES = 128
CAST_STEPS = 16
POOL_HALO = 2 * F32_SUBLANES
assert POOL_HALO >= POOL_STATE + 1

FFN_ROWS = 1024
FFN_SUB_ROWS = 256
FFN_SIDE_BY_SIDE = 2
MIX_ROWS = 512
MIX_SUB_ROWS = 256
ATTN_SAMPLE_STREAMS = 4

BF16 = jnp.bfloat16
F32 = jnp.float32


def _dot(a, b):
    return jnp.dot(a, b, preferred_element_type=F32)


def _rmsnorm(x, g):
    y = x * lax.rsqrt(jnp.mean(x * x, axis=-1, keepdims=True) + EPS)
    return y * g


def _layernorm(x, g):
    mu = jnp.mean(x, axis=-1, keepdims=True)
    xc = x - mu
    y = xc * lax.rsqrt(jnp.mean(xc * xc, axis=-1, keepdims=True) + EPS)
    return y * g


def _const_spec(shape):
    nd = len(shape)
    return pl.BlockSpec(shape, lambda *_: (0,) * nd, pipeline_mode=pl.Buffered(1))


def _params(*sem):
    return pltpu.CompilerParams(dimension_semantics=sem, vmem_limit_bytes=V7X_VMEM_LIMIT_BYTES)


def _cast_kernel(*refs):
    n = len(refs) // 2
    for src, dst in zip(refs[:n], refs[n:]):
        dst[...] = src[...].astype(BF16)


def _side_cast_specs(arrays, steps, flat_step):
    specs = []
    for a in arrays:
        rows = a.shape[0]
        n = next(n for n in range(steps, 0, -1) if rows % (n * BF16_SUBLANES) == 0)
        specs.append(pl.BlockSpec((rows // n, a.shape[1]),
                                  lambda *g, n=n: (jnp.minimum(flat_step(*g), n - 1), 0)))
    return specs


def _cast_bf16(arrays):
    for a in arrays:
        assert a.ndim == 2 and a.shape[0] % (CAST_STEPS * BF16_SUBLANES) == 0 and a.shape[1] % LANES == 0
    specs = [pl.BlockSpec((a.shape[0] // CAST_STEPS, a.shape[1]), lambda i: (i, 0)) for a in arrays]
    return pl.pallas_call(
        _cast_kernel,
        out_shape=tuple(jax.ShapeDtypeStruct(a.shape, BF16) for a in arrays),
        grid=(CAST_STEPS,),
        in_specs=specs,
        out_specs=tuple(specs),
        compiler_params=_params("arbitrary"),
        name="cast_weights",
    )(*arrays)


def _mem_kv_kernel(mem_ref, g_ref, wk_ref, wv_ref, k_ref, v_ref, kt_ref, vb_ref):
    mn = _rmsnorm(mem_ref[0], g_ref[...]).astype(BF16)
    k = _dot(mn, wk_ref[...])
    v = _dot(mn, wv_ref[...])
    for hd in range(MEM_HEADS):
        sl = slice(hd * MEM_HEAD_DIM, (hd + 1) * MEM_HEAD_DIM)
        k_ref[0, 0, :, hd, :] = k[:, sl]
        v_ref[0, 0, :, hd, :] = v[:, sl]
    kt_ref[0] = k.T.astype(BF16)
    vb_ref[0] = v.astype(BF16)


def _mem_kv(mem, g, wk, wv):
    nb = mem.shape[0]
    blk = pl.BlockSpec((1, N_MEM, D_MODEL), lambda b: (b, 0, 0))
    blk_t = pl.BlockSpec((1, D_MODEL, N_MEM), lambda b: (b, 0, 0))
    blk_heads = pl.BlockSpec((1, 1, N_MEM, MEM_HEADS, MEM_HEAD_DIM), lambda b: (0, b, 0, 0, 0))
    return pl.pallas_call(
        _mem_kv_kernel,
        out_shape=(jax.ShapeDtypeStruct((1, nb, N_MEM, MEM_HEADS, MEM_HEAD_DIM), F32),) * 2
        + (jax.ShapeDtypeStruct((nb, D_MODEL, N_MEM), BF16), jax.ShapeDtypeStruct((nb, N_MEM, D_MODEL), BF16)),
        grid=(nb,),
        in_specs=[blk, _const_spec((1, D_MODEL)), _const_spec((D_MODEL, D_MODEL)),
                  _const_spec((D_MODEL, D_MODEL))],
        out_specs=(blk_heads, blk_heads, blk_t, blk),
        compiler_params=_params("arbitrary"),
        name="mem_kv",
    )(mem, g, wk, wv)


def _run_side_by_side(stage_generators):
    live = list(stage_generators)
    while live:
        live = [g for g in live if next(g, StopIteration) is not StopIteration]


def _ffn_kernel(*refs, pre, final, sub, n_cast):
    refs = list(refs)
    x_ref = refs.pop(0)
    if pre:
        o_ref_in, wco_ref = refs.pop(0), refs.pop(0)
    g_ref, w1_ref, w3_ref, w2_ref = refs[:4]
    refs = refs[4:]
    if final:
        gf_ref = refs.pop(0)
    cast_in, out_ref, cast_out = refs[:n_cast], refs[n_cast], refs[n_cast + 1:]
    _cast_kernel(*cast_in, *cast_out)

    def sub_tile(r0):
        rows = slice(r0, r0 + sub)
        x = x_ref[rows, :]
        if pre:
            x = x + _dot(o_ref_in[rows, :], wco_ref[...])
        n = _rmsnorm(x, g_ref[...]).astype(BF16)
        yield
        a = _dot(n, w1_ref[...])
        yield
        b = _dot(n, w3_ref[...])
        yield
        mid = (jax.nn.silu(a) * b).astype(BF16)
        y = x + 0.5 * _dot(mid, w2_ref[...])
        if final:
            y = _rmsnorm(y, gf_ref[...])
        out_ref[rows, :] = y

    starts = list(range(0, x_ref.shape[0], sub))
    for i in range(0, len(starts), FFN_SIDE_BY_SIDE):
        _run_side_by_side([sub_tile(r0) for r0 in starts[i:i + FFN_SIDE_BY_SIDE]])


def _ffn(x, g, w1, w3, w2, *, o=None, wco=None, g_final=None, side_cast=(), name):
    rows = x.shape[0]
    tm = min(FFN_ROWS, rows)
    assert rows % tm == 0
    pre, final = o is not None, g_final is not None
    row_spec = pl.BlockSpec((tm, D_MODEL), lambda i: (i, 0))
    args, specs = [x], [row_spec]
    if pre:
        args += [o, wco]
        specs += [row_spec, _const_spec((D_MODEL, D_MODEL))]
    args += [g, w1, w3, w2]
    specs += [_const_spec((1, D_MODEL)), _const_spec((D_MODEL, D_FF)), _const_spec((D_MODEL, D_FF)),
              _const_spec((D_FF, D_MODEL))]
    if final:
        args.append(g_final)
        specs.append(_const_spec((1, D_MODEL)))
    cast_specs = _side_cast_specs(side_cast, rows // tm, lambda i: i)
    outs = pl.pallas_call(
        functools.partial(_ffn_kernel, pre=pre, final=final, sub=min(FFN_SUB_ROWS, tm), n_cast=len(side_cast)),
        out_shape=(jax.ShapeDtypeStruct((rows, D_MODEL), F32),)
        + tuple(jax.ShapeDtypeStruct(a.shape, BF16) for a in side_cast),
        grid=(rows // tm,),
        in_specs=specs + cast_specs,
        out_specs=(row_spec,) + tuple(cast_specs),
        compiler_params=_params("arbitrary"),
        name=name,
    )(*args, *side_cast)
    return outs if side_cast else outs[0]


def _in_proj(h, g_mix, w_in, b_gate):
    n = _rmsnorm(h, g_mix).astype(BF16)
    z = _dot(n, w_in)
    xa = z[:, :D_POOL]
    uv = jax.nn.gelu(z[:, D_POOL:D_POOL + 2 * D_SGU])
    gate = jax.nn.sigmoid(z[:, D_POOL + 2 * D_SGU:] + b_gate)
    return xa, uv[:, :D_SGU], uv[:, D_SGU:], gate[:, :D_MODEL], gate[:, D_MODEL:]


def _pool_project(pooled, xa, wpool_ref, pool_scale):
    mixed = []
    for g in range(POOL_GROUPS):
        sl = slice(g * POOL_GW, (g + 1) * POOL_GW)
        d = (pooled[g] - xa[:, sl]).astype(BF16)
        mixed.append(_dot(d, wpool_ref[g]))
    return jnp.concatenate(mixed, axis=-1) * pool_scale


def _merge(h, a, us, g_a, g_b, wpa_ref, wpb_ref, wo_ref):
    merged = g_a * _dot(a.astype(BF16), wpa_ref[...]) + g_b * _dot(us.astype(BF16), wpb_ref[...])
    return h + _dot(merged.astype(BF16), wo_ref[...])


def _softmax(sc):
    e = jnp.exp(sc - jnp.max(sc, axis=-1, keepdims=True))
    return e / jnp.sum(e, axis=-1, keepdims=True)


def _window_sums(halo, xa_g, window):
    s = jnp.concatenate([halo, xa_g], axis=0)
    step = 1
    while step < window:
        s = s + pltpu.roll(s, step, axis=0)
        step *= 2
    return s[POOL_HALO:]


def _mixer_prompt_kernel(h_ref, kt_ref, v_ref, gmix_ref, win_ref, bgate_ref, wpool_ref, pscale_ref,
                         gsgu_ref, ws_ref, bst_ref, wpa_ref, wpb_ref, wo_ref, gca_ref, wq_ref, wco_ref,
                         *rest, tm, sub, n_cast):
    cast_in, (out_ref, pool_ref), cast_out = rest[:n_cast], rest[n_cast:n_cast + 2], rest[n_cast + 2:-1]
    carry_ref = rest[-1]
    _cast_kernel(*cast_in, *cast_out)
    j = pl.program_id(1)

    @pl.when(j == 0)
    def _():
        carry_ref[...] = jnp.zeros((POOL_HALO, D_POOL), F32)

    blk_r = lax.broadcasted_iota(jnp.int32, (SGU_CHUNK, SGU_CHUNK), 0) // CHUNK
    blk_c = lax.broadcasted_iota(jnp.int32, (SGU_CHUNK, SGU_CHUNK), 1) // CHUNK
    wm = [jnp.where(blk_r >= blk_c, ws_ref[hd], 0.0).astype(BF16) for hd in range(SGU_HEADS)]
    n_chunks = sub // SGU_CHUNK

    n_sub = tm // sub
    halos = [carry_ref[...]] + [None] * n_sub

    def sub_tile(i):
        r0 = i * sub
        h = h_ref[0, r0:r0 + sub, :]
        xa, u, v, g_a, g_b = _in_proj(h, gmix_ref[...], win_ref[...], bgate_ref[...])
        halos[i + 1] = xa[sub - POOL_HALO:, :]
        yield

        pos = j * tm + r0 + lax.broadcasted_iota(jnp.int32, (sub, 1), 0)
        pooled = []
        for g, w in enumerate(POOL_WINDOWS):
            sl = slice(g * POOL_GW, (g + 1) * POOL_GW)
            cnt = jnp.minimum(w, pos + 1).astype(F32)
            pooled.append(_window_sums(halos[i][:, sl], xa[:, sl], w) / cnt)
        a = _pool_project(pooled, xa, wpool_ref, pscale_ref[...])
        yield

        vn = _layernorm(v, gsgu_ref[...]).astype(BF16)
        s_heads = []
        for hd in range(SGU_HEADS):
            cols = slice(hd * SGU_HW, (hd + 1) * SGU_HW)
            vcat = jnp.concatenate([vn[c * SGU_CHUNK:(c + 1) * SGU_CHUNK, cols] for c in range(n_chunks)],
                                   axis=-1)
            sh = _dot(wm[hd], vcat) + bst_ref[:, hd:hd + 1]
            s_heads.append(jnp.concatenate([sh[:, c * SGU_HW:(c + 1) * SGU_HW] for c in range(n_chunks)],
                                           axis=0))
        s = jnp.concatenate(s_heads, axis=-1)
        yield

        h = _merge(h, a, u * s, g_a, g_b, wpa_ref, wpb_ref, wo_ref)
        yield
        q = _dot(_rmsnorm(h, gca_ref[...]).astype(BF16), wq_ref[...]).astype(BF16)
        yield
        head = [slice(hd * MEM_HEAD_DIM, (hd + 1) * MEM_HEAD_DIM) for hd in range(MEM_HEADS)]
        sc = [_dot(q[:, sl], kt_ref[0, sl, :]) * (MEM_HEAD_DIM ** -0.5) for sl in head]
        yield
        p = [_softmax(x).astype(BF16) for x in sc]
        yield
        o = jnp.concatenate([_dot(p[hd], v_ref[0, :, head[hd]]) for hd in range(MEM_HEADS)], axis=-1)
        out_ref[0, r0:r0 + sub, :] = h + _dot(o.astype(BF16), wco_ref[...])

    _run_side_by_side([sub_tile(i) for i in range(n_sub)])

    carry_ref[...] = halos[n_sub]

    @pl.when(j == pl.num_programs(1) - 1)
    def _():
        pool_ref[0] = halos[n_sub]


def _mixer_prompt(h, kb, vb, w, side_cast=()):
    nb, seq, _ = h.shape
    tm, sub = MIX_ROWS, MIX_SUB_ROWS
    assert seq % tm == 0 and tm % sub == 0 and sub % SGU_CHUNK == 0
    row_spec = pl.BlockSpec((1, tm, D_MODEL), lambda b, j: (b, j, 0))
    kt_spec = pl.BlockSpec((1, D_MODEL, N_MEM), lambda b, j: (b, 0, 0))
    v_spec = pl.BlockSpec((1, N_MEM, D_MODEL), lambda b, j: (b, 0, 0))
    consts = [w["g_mix"], w["w_in"], w["b_gate"], w["w_pool"], w["pool_scale"], w["g_sgu"], w["w_s"],
              w["b_s_t"], w["w_pa"], w["w_pb"], w["w_o"], w["g_ca"], w["w_q"], w["w_co"]]
    tiles = seq // tm
    cast_specs = _side_cast_specs(side_cast, nb * tiles, lambda b, j: b * tiles + j)
    return pl.pallas_call(
        functools.partial(_mixer_prompt_kernel, tm=tm, sub=sub, n_cast=len(side_cast)),
        out_shape=(jax.ShapeDtypeStruct((nb, seq, D_MODEL), F32),
                   jax.ShapeDtypeStruct((nb, POOL_HALO, D_POOL), F32))
        + tuple(jax.ShapeDtypeStruct(a.shape, BF16) for a in side_cast),
        grid=(nb, tiles),
        in_specs=[row_spec, kt_spec, v_spec] + [_const_spec(c.shape) for c in consts] + cast_specs,
        out_specs=(row_spec, pl.BlockSpec((1, POOL_HALO, D_POOL), lambda b, j: (b, 0, 0))) + tuple(cast_specs),
        scratch_shapes=[pltpu.VMEM((POOL_HALO, D_POOL), F32)],
        compiler_params=_params("arbitrary", "arbitrary"),
        name="mixer_prompt",
    )(h, kb, vb, *consts, *side_cast)


def _mixer_sample_kernel(h_ref, state_ref, gmix_ref, win_ref, bgate_ref, wpool_ref, pscale_ref,
                         gsgu_ref, wsx_ref, bsx_ref, wpa_ref, wpb_ref, wo_ref, gca_ref, wq_ref,
                         out_ref, q_ref, pool_ref, vn_ref, *, nb, t, past_len):
    rows = t * nb
    halo = POOL_STATE * nb
    h = h_ref[...]
    xa, u, v, g_a, g_b = _in_proj(h, gmix_ref[...], win_ref[...], bgate_ref[...])

    cat = jnp.concatenate([state_ref[...], xa], axis=0)
    pooled = []
    for g, w in enumerate(POOL_WINDOWS):
        sl = slice(g * POOL_GW, (g + 1) * POOL_GW)
        acc = xa[:, sl]
        for back in range(1, w):
            acc = acc + cat[halo - back * nb:halo - back * nb + rows, sl]
        pos = past_len + lax.broadcasted_iota(jnp.int32, (rows, 1), 0) // nb
        cnt = jnp.minimum(w, pos + 1).astype(F32)
        pooled.append(acc / cnt)
    a = _pool_project(pooled, xa, wpool_ref, pscale_ref[...])
    pool_ref[...] = cat[rows:rows + halo, :]

    vn = _layernorm(v, gsgu_ref[...])
    vn_ref[...] = vn
    s_rows = []
    for p in range(t):
        acc = jnp.broadcast_to(bsx_ref[p:p + 1, :], (nb, D_SGU))
        for qq in range(t):
            if p // CHUNK >= qq // CHUNK:
                acc = acc + wsx_ref[p, qq:qq + 1, :] * vn[qq * nb:(qq + 1) * nb, :]
        s_rows.append(acc)
    s = jnp.concatenate(s_rows, axis=0)

    h = _merge(h, a, u * s, g_a, g_b, wpa_ref, wpb_ref, wo_ref)
    out_ref[...] = h
    q_ref[...] = _dot(_rmsnorm(h, gca_ref[...]).astype(BF16), wq_ref[...]).astype(BF16)


def _mixer_sample(h, state, w, *, nb, t, past_len):
    rows = nb * t
    consts = [w["g_mix"], w["w_in"], w["b_gate"], w["w_pool"], w["pool_scale"], w["g_sgu"], w["w_s_x"],
              w["b_s_x"], w["w_pa"], w["w_pb"], w["w_o"], w["g_ca"], w["w_q"]]
    args = [h, state] + consts
    return pl.pallas_call(
        functools.partial(_mixer_sample_kernel, nb=nb, t=t, past_len=past_len),
        out_shape=(jax.ShapeDtypeStruct((rows, D_MODEL), F32),
                   jax.ShapeDtypeStruct((rows, D_MODEL), BF16),
                   jax.ShapeDtypeStruct((POOL_STATE * nb, D_POOL), F32),
                   jax.ShapeDtypeStruct((rows, D_SGU), F32)),
        grid=(1,),
        in_specs=[_const_spec(a.shape) for a in args],
        out_specs=(_const_spec((rows, D_MODEL)), _const_spec((rows, D_MODEL)),
                   _const_spec((POOL_STATE * nb, D_POOL)), _const_spec((rows, D_SGU))),
        compiler_params=_params("arbitrary"),
        name="mixer_sample",
    )(*args)


def _attn_sample_kernel(q_ref, k_ref, v_ref, o_ref, *, streams):
    t = q_ref.shape[1]
    n_cols = N_MEM * MEM_HEADS
    row_head = lax.broadcasted_iota(jnp.int32, (MEM_HEADS * t, n_cols), 0) // t
    col_head = lax.broadcasted_iota(jnp.int32, (MEM_HEADS * t, n_cols), 1) % MEM_HEADS
    own_head = row_head == col_head

    def stream(s):
        q = q_ref[s]
        q_rows = jnp.concatenate([q[:, hd * MEM_HEAD_DIM:(hd + 1) * MEM_HEAD_DIM] for hd in range(MEM_HEADS)],
                                 axis=0)
        k = k_ref[0, s].reshape(n_cols, MEM_HEAD_DIM).astype(BF16)
        v = v_ref[0, s].reshape(n_cols, MEM_HEAD_DIM).astype(BF16)
        yield
        sc = lax.dot_general(q_rows, k, (((1,), (1,)), ((), ())), preferred_element_type=F32)
        sc = jnp.where(own_head, sc * (MEM_HEAD_DIM ** -0.5), -jnp.inf)
        yield
        p = _softmax(sc)
        yield
        o = _dot(p.astype(BF16), v)
        o_ref[s] = jnp.concatenate([o[hd * t:(hd + 1) * t, :] for hd in range(MEM_HEADS)],
                                   axis=-1).astype(BF16)

    _run_side_by_side([stream(s) for s in range(streams)])


def _attn_sample(q, k, v):
    nb, t, _ = q.shape
    streams = ATTN_SAMPLE_STREAMS
    assert nb % streams == 0
    q_spec = pl.BlockSpec((streams, t, D_MODEL), lambda b: (b, 0, 0))
    kv_spec = pl.BlockSpec((1, streams, N_MEM, MEM_HEADS, MEM_HEAD_DIM), lambda b: (0, b, 0, 0, 0))
    return pl.pallas_call(
        functools.partial(_attn_sample_kernel, streams=streams),
        out_shape=jax.ShapeDtypeStruct((nb, t, D_MODEL), BF16),
        grid=(nb // streams,),
        in_specs=[q_spec, kv_spec, kv_spec],
        out_specs=q_spec,
        compiler_params=_params("arbitrary"),
        name="attn_sample",
    )(q, k, v)


def kernel(x_prompt, x_sample, state_pool, cache_mem_k, cache_mem_v, mem_prompt, g_ff1, w1a, w3a, w2a,
           g_mix, w_in, b_gate, w_pool, pool_scale, g_sgu, w_s, b_s, w_pa, w_pb, w_o, g_mem, w_mk, w_mv,
           g_ca, w_q, w_co, g_ff2, w1b, w3b, w2b, g_final):
    nb, seq, _ = x_prompt.shape
    nbs, t, _ = x_sample.shape
    depth = g_ff1.shape[0]
    assert depth == 1
    l = 0
    row = lambda a: a.reshape(1, -1)
    mat = lambda a: a[l].reshape(-1, a.shape[-1])
    w1a_b, w3a_b, w2a_b, w_mk_b, w_mv_b = _cast_bf16([mat(a) for a in (w1a, w3a, w2a, w_mk, w_mv)])
    mk, mv, mk_t, mv_b = _mem_kv(mem_prompt, row(g_mem[l]), w_mk_b, w_mv_b)
    hp, w_in_b, w_pa_b, w_pb_b, w_o_b, w_q_b, w_co_b, w_pool_b = _ffn(
        x_prompt.reshape(nb * seq, D_MODEL), row(g_ff1[l]), w1a_b, w3a_b, w2a_b,
        side_cast=[mat(a) for a in (w_in, w_pa, w_pb, w_o, w_q, w_co, w_pool)], name="ffn1_prompt")
    w = dict(
        g_mix=row(g_mix[l]), w_in=w_in_b, b_gate=row(b_gate[l]),
        w_pool=w_pool_b.reshape(POOL_GROUPS, POOL_GW, POOL_GW),
        pool_scale=row(pool_scale[l]), g_sgu=row(g_sgu[l]), w_s=w_s[l], b_s_t=b_s[l].T,
        w_pa=w_pa_b, w_pb=w_pb_b, w_o=w_o_b, g_ca=row(g_ca[l]), w_q=w_q_b, w_co=w_co_b,
        w_s_x=jnp.repeat(jnp.transpose(w_s[l][:, :t, :t], (1, 2, 0)), SGU_HW, axis=-1),
        b_s_x=jnp.repeat(b_s[l][:, :t].T, SGU_HW, axis=-1),
    )

    hp, pool_p, w1b_b, w3b_b, w2b_b = _mixer_prompt(hp.reshape(nb, seq, D_MODEL), mk_t, mv_b, w,
                                                    side_cast=[mat(a) for a in (w1b, w3b, w2b)])
    y_prompt = _ffn(hp.reshape(nb * seq, D_MODEL), row(g_ff2[l]), w1b_b, w3b_b, w2b_b,
                    g_final=row(g_final), name="ffn2_prompt").reshape(nb, seq, D_MODEL)

    xs = jnp.transpose(x_sample, (1, 0, 2)).reshape(t * nbs, D_MODEL)
    st = jnp.transpose(state_pool[l], (1, 0, 2)).reshape(POOL_STATE * nbs, D_POOL)
    hs = _ffn(xs, row(g_ff1[l]), w1a_b, w3a_b, w2a_b, name="ffn1_sample")
    hs, qs, pool_s, vn_s = _mixer_sample(hs, st, w, nb=nbs, t=t, past_len=PAST_LEN)
    to_stream_major = lambda a, n: jnp.transpose(a.reshape(n, nbs, a.shape[-1]), (1, 0, 2))
    hs = to_stream_major(hs, t)
    qs = to_stream_major(qs, t)
    os_ = _attn_sample(qs, cache_mem_k, cache_mem_v)
    y_sample = _ffn(hs.reshape(nbs * t, D_MODEL), row(g_ff2[l]), w1b_b, w3b_b, w2b_b,
                    o=os_.reshape(nbs * t, D_MODEL), wco=w["w_co"], g_final=row(g_final),
                    name="ffn2_sample").reshape(nbs, t, D_MODEL)

    pool_prompt = pool_p[:, POOL_HALO - POOL_STATE:, :][None]
    pool_sample = to_stream_major(pool_s, POOL_STATE)[None]
    sgu_v_sample = to_stream_major(vn_s, t)[None]
    return (y_prompt, y_sample, pool_prompt, pool_sample, sgu_v_sample, mk, mv)
```

```python
import functools

import jax
import jax.numpy as jnp
from jax import lax
from jax.experimental import pallas as pl
from jax.experimental.pallas import tpu as pltpu

D_MODEL = 1024
PAST_LEN = 1024
CHUNK = 64
N_MEM = 256
MEM_HEADS = 4
MEM_HEAD_DIM = D_MODEL // MEM_HEADS
D_POOL = D_MODEL // 2
POOL_WINDOWS = (2, 4, 8, 16)
POOL_GROUPS = len(POOL_WINDOWS)
POOL_GW = D_POOL // POOL_GROUPS
POOL_STATE = max(POOL_WINDOWS) - 1
D_SGU = D_MODEL // 2
SGU_HEADS = 4
SGU_HW = D_SGU // SGU_HEADS
SGU_CHUNK = 128
D_FF = 2816
D_IN = D_POOL + 2 * D_SGU + 2 * D_MODEL
EPS = 1e-6

V7X_VMEM_LIMIT_BYTES = 56 * 1024 * 1024
F32_SUBLANES = 8
BF16_SUBLANES = 16
LANES = 128
CAST_STEPS = 16
POOL_HALO = 2 * F32_SUBLANES
assert POOL_HALO >= POOL_STATE + 1

FFN_ROWS = 1024
FFN_SUB_ROWS = 256
FFN_SIDE_BY_SIDE = 2
MIX_ROWS = 1024
MIX_SUB_ROWS = 256
ATTN_SAMPLE_STREAMS = 4

BF16 = jnp.bfloat16
F32 = jnp.float32


def _dot(a, b):
    return jnp.dot(a, b, preferred_element_type=F32)


def _rmsnorm(x, g):
    y = x * lax.rsqrt(jnp.mean(x * x, axis=-1, keepdims=True) + EPS)
    return y * g


def _layernorm(x, g):
    mu = jnp.mean(x, axis=-1, keepdims=True)
    xc = x - mu
    y = xc * lax.rsqrt(jnp.mean(xc * xc, axis=-1, keepdims=True) + EPS)
    return y * g


def _const_spec(shape):
    nd = len(shape)
    return pl.BlockSpec(shape, lambda *_: (0,) * nd, pipeline_mode=pl.Buffered(1))


def _params(*sem):
    return pltpu.CompilerParams(dimension_semantics=sem, vmem_limit_bytes=V7X_VMEM_LIMIT_BYTES)


def _cast_kernel(*refs):
    n = len(refs) // 2
    for src, dst in zip(refs[:n], refs[n:]):
        dst[...] = src[...].astype(BF16)


def _side_cast_specs(arrays, steps, flat_step):
    specs = []
    for a in arrays:
        rows = a.shape[0]
        n = next(n for n in range(steps, 0, -1) if rows % (n * BF16_SUBLANES) == 0)
        specs.append(pl.BlockSpec((rows // n, a.shape[1]),
                                  lambda *g, n=n: (jnp.minimum(flat_step(*g), n - 1), 0)))
    return specs


def _cast_bf16(arrays):
    for a in arrays:
        assert a.ndim == 2 and a.shape[0] % (CAST_STEPS * BF16_SUBLANES) == 0 and a.shape[1] % LANES == 0
    specs = [pl.BlockSpec((a.shape[0] // CAST_STEPS, a.shape[1]), lambda i: (i, 0)) for a in arrays]
    return pl.pallas_call(
        _cast_kernel,
        out_shape=tuple(jax.ShapeDtypeStruct(a.shape, BF16) for a in arrays),
        grid=(CAST_STEPS,),
        in_specs=specs,
        out_specs=tuple(specs),
        compiler_params=_params("arbitrary"),
        name="cast_weights",
    )(*arrays)


def _mem_kv_kernel(mem_ref, g_ref, wk_ref, wv_ref, k_ref, v_ref, kt_ref, vb_ref):
    mn = _rmsnorm(mem_ref[0], g_ref[...]).astype(BF16)
    k = _dot(mn, wk_ref[...])
    v = _dot(mn, wv_ref[...])
    for hd in range(MEM_HEADS):
        sl = slice(hd * MEM_HEAD_DIM, (hd + 1) * MEM_HEAD_DIM)
        k_ref[0, 0, :, hd, :] = k[:, sl]
        v_ref[0, 0, :, hd, :] = v[:, sl]
    kt_ref[0] = k.T.astype(BF16)
    vb_ref[0] = v.astype(BF16)


def _mem_kv(mem, g, wk, wv):
    nb = mem.shape[0]
    blk = pl.BlockSpec((1, N_MEM, D_MODEL), lambda b: (b, 0, 0))
    blk_t = pl.BlockSpec((1, D_MODEL, N_MEM), lambda b: (b, 0, 0))
    blk_heads = pl.BlockSpec((1, 1, N_MEM, MEM_HEADS, MEM_HEAD_DIM), lambda b: (0, b, 0, 0, 0))
    return pl.pallas_call(
        _mem_kv_kernel,
        out_shape=(jax.ShapeDtypeStruct((1, nb, N_MEM, MEM_HEADS, MEM_HEAD_DIM), F32),) * 2
        + (jax.ShapeDtypeStruct((nb, D_MODEL, N_MEM), BF16), jax.ShapeDtypeStruct((nb, N_MEM, D_MODEL), BF16)),
        grid=(nb,),
        in_specs=[blk, _const_spec((1, D_MODEL)), _const_spec((D_MODEL, D_MODEL)),
                  _const_spec((D_MODEL, D_MODEL))],
        out_specs=(blk_heads, blk_heads, blk_t, blk),
        compiler_params=_params("arbitrary"),
        name="mem_kv",
    )(mem, g, wk, wv)


def _run_side_by_side(stage_generators):
    live = list(stage_generators)
    while live:
        live = [g for g in live if next(g, StopIteration) is not StopIteration]


def _ffn_kernel(*refs, pre, final, sub, n_cast):
    refs = list(refs)
    x_ref = refs.pop(0)
    if pre:
        o_ref_in, wco_ref = refs.pop(0), refs.pop(0)
    g_ref, w1_ref, w3_ref, w2_ref = refs[:4]
    refs = refs[4:]
    if final:
        gf_ref = refs.pop(0)
    cast_in, out_ref, cast_out = refs[:n_cast], refs[n_cast], refs[n_cast + 1:]
    _cast_kernel(*cast_in, *cast_out)

    def sub_tile(r0):
        rows = slice(r0, r0 + sub)
        x = x_ref[rows, :]
        if pre:
            x = x + _dot(o_ref_in[rows, :], wco_ref[...])
        n = _rmsnorm(x, g_ref[...]).astype(BF16)
        yield
        a = _dot(n, w1_ref[...])
        yield
        b = _dot(n, w3_ref[...])
        yield
        mid = (jax.nn.silu(a) * b).astype(BF16)
        y = x + 0.5 * _dot(mid, w2_ref[...])
        if final:
            y = _rmsnorm(y, gf_ref[...])
        out_ref[rows, :] = y

    starts = list(range(0, x_ref.shape[0], sub))
    for i in range(0, len(starts), FFN_SIDE_BY_SIDE):
        _run_side_by_side([sub_tile(r0) for r0 in starts[i:i + FFN_SIDE_BY_SIDE]])


def _ffn(x, g, w1, w3, w2, *, o=None, wco=None, g_final=None, side_cast=(), name):
    rows = x.shape[0]
    tm = min(FFN_ROWS, rows)
    assert rows % tm == 0
    pre, final = o is not None, g_final is not None
    row_spec = pl.BlockSpec((tm, D_MODEL), lambda i: (i, 0))
    args, specs = [x], [row_spec]
    if pre:
        args += [o, wco]
        specs += [row_spec, _const_spec((D_MODEL, D_MODEL))]
    args += [g, w1, w3, w2]
    specs += [_const_spec((1, D_MODEL)), _const_spec((D_MODEL, D_FF)), _const_spec((D_MODEL, D_FF)),
              _const_spec((D_FF, D_MODEL))]
    if final:
        args.append(g_final)
        specs.append(_const_spec((1, D_MODEL)))
    cast_specs = _side_cast_specs(side_cast, rows // tm, lambda i: i)
    outs = pl.pallas_call(
        functools.partial(_ffn_kernel, pre=pre, final=final, sub=min(FFN_SUB_ROWS, tm), n_cast=len(side_cast)),
        out_shape=(jax.ShapeDtypeStruct((rows, D_MODEL), F32),)
        + tuple(jax.ShapeDtypeStruct(a.shape, BF16) for a in side_cast),
        grid=(rows // tm,),
        in_specs=specs + cast_specs,
        out_specs=(row_spec,) + tuple(cast_specs),
        compiler_params=_params("arbitrary"),
        name=name,
    )(*args, *side_cast)
    return outs if side_cast else outs[0]


def _in_proj(h, g_mix, w_in, b_gate):
    n = _rmsnorm(h, g_mix).astype(BF16)
    z = _dot(n, w_in)
    xa = z[:, :D_POOL]
    uv = jax.nn.gelu(z[:, D_POOL:D_POOL + 2 * D_SGU])
    gate = jax.nn.sigmoid(z[:, D_POOL + 2 * D_SGU:] + b_gate)
    return xa, uv[:, :D_SGU], uv[:, D_SGU:], gate[:, :D_MODEL], gate[:, D_MODEL:]


def _pool_project(pooled, xa, wpool_ref, pool_scale):
    mixed = []
    for g in range(POOL_GROUPS):
        sl = slice(g * POOL_GW, (g + 1) * POOL_GW)
        d = (pooled[g] - xa[:, sl]).astype(BF16)
        mixed.append(_dot(d, wpool_ref[g]))
    return jnp.concatenate(mixed, axis=-1) * pool_scale


def _merge(h, a, us, g_a, g_b, wpa_ref, wpb_ref, wo_ref):
    merged = g_a * _dot(a.astype(BF16), wpa_ref[...]) + g_b * _dot(us.astype(BF16), wpb_ref[...])
    return h + _dot(merged.astype(BF16), wo_ref[...])


def _softmax(sc):
    e = jnp.exp(sc - jnp.max(sc, axis=-1, keepdims=True))
    return e / jnp.sum(e, axis=-1, keepdims=True)


def _window_sums(halo, xa_g, window):
    s = jnp.concatenate([halo, xa_g], axis=0)
    step = 1
    while step < window:
        s = s + pltpu.roll(s, step, axis=0)
        step *= 2
    return s[POOL_HALO:]


def _mixer_prompt_kernel(h_ref, kt_ref, v_ref, gmix_ref, win_ref, bgate_ref, wpool_ref, pscale_ref,
                         gsgu_ref, ws_ref, bst_ref, wpa_ref, wpb_ref, wo_ref, gca_ref, wq_ref, wco_ref,
                         *rest, tm, sub, n_cast):
    cast_in, (out_ref, pool_ref), cast_out = rest[:n_cast], rest[n_cast:n_cast + 2], rest[n_cast + 2:-1]
    carry_ref = rest[-1]
    _cast_kernel(*cast_in, *cast_out)
    j = pl.program_id(1)

    @pl.when(j == 0)
    def _():
        carry_ref[...] = jnp.zeros((POOL_HALO, D_POOL), F32)

    blk_r = lax.broadcasted_iota(jnp.int32, (SGU_CHUNK, SGU_CHUNK), 0) // CHUNK
    blk_c = lax.broadcasted_iota(jnp.int32, (SGU_CHUNK, SGU_CHUNK), 1) // CHUNK
    wm = [jnp.where(blk_r >= blk_c, ws_ref[hd], 0.0).astype(BF16) for hd in range(SGU_HEADS)]
    n_chunks = sub // SGU_CHUNK

    n_sub = tm // sub
    halos = [carry_ref[...]] + [None] * n_sub

    def sub_tile(i):
        r0 = i * sub
        h = h_ref[0, r0:r0 + sub, :]
        xa, u, v, g_a, g_b = _in_proj(h, gmix_ref[...], win_ref[...], bgate_ref[...])
        halos[i + 1] = xa[sub - POOL_HALO:, :]
        yield

        pos = j * tm + r0 + lax.broadcasted_iota(jnp.int32, (sub, 1), 0)
        pooled = []
        for g, w in enumerate(POOL_WINDOWS):
            sl = slice(g * POOL_GW, (g + 1) * POOL_GW)
            cnt = jnp.minimum(w, pos + 1).astype(F32)
            pooled.append(_window_sums(halos[i][:, sl], xa[:, sl], w) / cnt)
        a = _pool_project(pooled, xa, wpool_ref, pscale_ref[...])
        yield

        vn = _layernorm(v, gsgu_ref[...]).astype(BF16)
        s_heads = []
        for hd in range(SGU_HEADS):
            cols = slice(hd * SGU_HW, (hd + 1) * SGU_HW)
            vcat = jnp.concatenate([vn[c * SGU_CHUNK:(c + 1) * SGU_CHUNK, cols] for c in range(n_chunks)],
                                   axis=-1)
            sh = _dot(wm[hd], vcat) + bst_ref[:, hd:hd + 1]
            s_heads.append(jnp.concatenate([sh[:, c * SGU_HW:(c + 1) * SGU_HW] for c in range(n_chunks)],
                                           axis=0))
        s = jnp.concatenate(s_heads, axis=-1)
        yield

        h = _merge(h, a, u * s, g_a, g_b, wpa_ref, wpb_ref, wo_ref)
        yield
        q = _dot(_rmsnorm(h, gca_ref[...]).astype(BF16), wq_ref[...]).astype(BF16)
        yield
        head = [slice(hd * MEM_HEAD_DIM, (hd + 1) * MEM_HEAD_DIM) for hd in range(MEM_HEADS)]
        sc = [_dot(q[:, sl], kt_ref[0, sl, :]) * (MEM_HEAD_DIM ** -0.5) for sl in head]
        yield
        p = [_softmax(x).astype(BF16) for x in sc]
        yield
        o = jnp.concatenate([_dot(p[hd], v_ref[0, :, head[hd]]) for hd in range(MEM_HEADS)], axis=-1)
        out_ref[0, r0:r0 + sub, :] = h + _dot(o.astype(BF16), wco_ref[...])

    _run_side_by_side([sub_tile(i) for i in range(n_sub)])

    carry_ref[...] = halos[n_sub]

    @pl.when(j == pl.num_programs(1) - 1)
    def _():
        pool_ref[0] = halos[n_sub]


def _mixer_prompt(h, kb, vb, w, side_cast=()):
    nb, seq, _ = h.shape
    tm, sub = MIX_ROWS, MIX_SUB_ROWS
    assert seq % tm == 0 and tm % sub == 0 and sub % SGU_CHUNK == 0
    row_spec = pl.BlockSpec((1, tm, D_MODEL), lambda b, j: (b, j, 0))
    kt_spec = pl.BlockSpec((1, D_MODEL, N_MEM), lambda b, j: (b, 0, 0))
    v_spec = pl.BlockSpec((1, N_MEM, D_MODEL), lambda b, j: (b, 0, 0))
    consts = [w["g_mix"], w["w_in"], w["b_gate"], w["w_pool"], w["pool_scale"], w["g_sgu"], w["w_s"],
              w["b_s_t"], w["w_pa"], w["w_pb"], w["w_o"], w["g_ca"], w["w_q"], w["w_co"]]
    tiles = seq // tm
    cast_specs = _side_cast_specs(side_cast, nb * tiles, lambda b, j: b * tiles + j)
    return pl.pallas_call(
        functools.partial(_mixer_prompt_kernel, tm=tm, sub=sub, n_cast=len(side_cast)),
        out_shape=(jax.ShapeDtypeStruct((nb, seq, D_MODEL), F32),
                   jax.ShapeDtypeStruct((nb, POOL_HALO, D_POOL), F32))
        + tuple(jax.ShapeDtypeStruct(a.shape, BF16) for a in side_cast),
        grid=(nb, tiles),
        in_specs=[row_spec, kt_spec, v_spec] + [_const_spec(c.shape) for c in consts] + cast_specs,
        out_specs=(row_spec, pl.BlockSpec((1, POOL_HALO, D_POOL), lambda b, j: (b, 0, 0))) + tuple(cast_specs),
        scratch_shapes=[pltpu.VMEM((POOL_HALO, D_POOL), F32)],
        compiler_params=_params("arbitrary", "arbitrary"),
        name="mixer_prompt",
    )(h, kb, vb, *consts, *side_cast)


def _mixer_sample_kernel(h_ref, state_ref, gmix_ref, win_ref, bgate_ref, wpool_ref, pscale_ref,
                         gsgu_ref, wsx_ref, bsx_ref, wpa_ref, wpb_ref, wo_ref, gca_ref, wq_ref,
                         out_ref, q_ref, pool_ref, vn_ref, *, nb, t, past_len):
    rows = t * nb
    halo = POOL_STATE * nb
    h = h_ref[...]
    xa, u, v, g_a, g_b = _in_proj(h, gmix_ref[...], win_ref[...], bgate_ref[...])

    cat = jnp.concatenate([state_ref[...], xa], axis=0)
    pooled = []
    for g, w in enumerate(POOL_WINDOWS):
        sl = slice(g * POOL_GW, (g + 1) * POOL_GW)
        acc = xa[:, sl]
        for back in range(1, w):
            acc = acc + cat[halo - back * nb:halo - back * nb + rows, sl]
        pos = past_len + lax.broadcasted_iota(jnp.int32, (rows, 1), 0) // nb
        cnt = jnp.minimum(w, pos + 1).astype(F32)
        pooled.append(acc / cnt)
    a = _pool_project(pooled, xa, wpool_ref, pscale_ref[...])
    pool_ref[...] = cat[rows:rows + halo, :]

    vn = _layernorm(v, gsgu_ref[...])
    vn_ref[...] = vn
    s_rows = []
    for p in range(t):
        acc = jnp.broadcast_to(bsx_ref[p:p + 1, :], (nb, D_SGU))
        for qq in range(t):
            if p // CHUNK >= qq // CHUNK:
                acc = acc + wsx_ref[p, qq:qq + 1, :] * vn[qq * nb:(qq + 1) * nb, :]
        s_rows.append(acc)
    s = jnp.concatenate(s_rows, axis=0)

    h = _merge(h, a, u * s, g_a, g_b, wpa_ref, wpb_ref, wo_ref)
    out_ref[...] = h
    q_ref[...] = _dot(_rmsnorm(h, gca_ref[...]).astype(BF16), wq_ref[...]).astype(BF16)


def _mixer_sample(h, state, w, *, nb, t, past_len):
    rows = nb * t
    consts = [w["g_mix"], w["w_in"], w["b_gate"], w["w_pool"], w["pool_scale"], w["g_sgu"], w["w_s_x"],
              w["b_s_x"], w["w_pa"], w["w_pb"], w["w_o"], w["g_ca"], w["w_q"]]
    args = [h, state] + consts
    return pl.pallas_call(
        functools.partial(_mixer_sample_kernel, nb=nb, t=t, past_len=past_len),
        out_shape=(jax.ShapeDtypeStruct((rows, D_MODEL), F32),
                   jax.ShapeDtypeStruct((rows, D_MODEL), BF16),
                   jax.ShapeDtypeStruct((POOL_STATE * nb, D_POOL), F32),
                   jax.ShapeDtypeStruct((rows, D_SGU), F32)),
        grid=(1,),
        in_specs=[_const_spec(a.shape) for a in args],
        out_specs=(_const_spec((rows, D_MODEL)), _const_spec((rows, D_MODEL)),
                   _const_spec((POOL_STATE * nb, D_POOL)), _const_spec((rows, D_SGU))),
        compiler_params=_params("arbitrary"),
        name="mixer_sample",
    )(*args)


def _attn_sample_kernel(q_ref, k_ref, v_ref, o_ref, *, streams):
    t = q_ref.shape[1]
    n_cols = N_MEM * MEM_HEADS
    row_head = lax.broadcasted_iota(jnp.int32, (MEM_HEADS * t, n_cols), 0) // t
    col_head = lax.broadcasted_iota(jnp.int32, (MEM_HEADS * t, n_cols), 1) % MEM_HEADS
    own_head = row_head == col_head

    def stream(s):
        q = q_ref[s]
        q_rows = jnp.concatenate([q[:, hd * MEM_HEAD_DIM:(hd + 1) * MEM_HEAD_DIM] for hd in range(MEM_HEADS)],
                                 axis=0)
        k = k_ref[0, s].reshape(n_cols, MEM_HEAD_DIM).astype(BF16)
        v = v_ref[0, s].reshape(n_cols, MEM_HEAD_DIM).astype(BF16)
        yield
        sc = lax.dot_general(q_rows, k, (((1,), (1,)), ((), ())), preferred_element_type=F32)
        sc = jnp.where(own_head, sc * (MEM_HEAD_DIM ** -0.5), -jnp.inf)
        yield
        p = _softmax(sc)
        yield
        o = _dot(p.astype(BF16), v)
        o_ref[s] = jnp.concatenate([o[hd * t:(hd + 1) * t, :] for hd in range(MEM_HEADS)],
                                   axis=-1).astype(BF16)

    _run_side_by_side([stream(s) for s in range(streams)])


def _attn_sample(q, k, v):
    nb, t, _ = q.shape
    streams = ATTN_SAMPLE_STREAMS
    assert nb % streams == 0
    q_spec = pl.BlockSpec((streams, t, D_MODEL), lambda b: (b, 0, 0))
    kv_spec = pl.BlockSpec((1, streams, N_MEM, MEM_HEADS, MEM_HEAD_DIM), lambda b: (0, b, 0, 0, 0))
    return pl.pallas_call(
        functools.partial(_attn_sample_kernel, streams=streams),
        out_shape=jax.ShapeDtypeStruct((nb, t, D_MODEL), BF16),
        grid=(nb // streams,),
        in_specs=[q_spec, kv_spec, kv_spec],
        out_specs=q_spec,
        compiler_params=_params("arbitrary"),
        name="attn_sample",
    )(q, k, v)


def kernel(x_prompt, x_sample, state_pool, cache_mem_k, cache_mem_v, mem_prompt, g_ff1, w1a, w3a, w2a,
           g_mix, w_in, b_gate, w_pool, pool_scale, g_sgu, w_s, b_s, w_pa, w_pb, w_o, g_mem, w_mk, w_mv,
           g_ca, w_q, w_co, g_ff2, w1b, w3b, w2b, g_final):
    nb, seq, _ = x_prompt.shape
    nbs, t, _ = x_sample.shape
    depth = g_ff1.shape[0]
    assert depth == 1
    l = 0
    row = lambda a: a.reshape(1, -1)
    mat = lambda a: a[l].reshape(-1, a.shape[-1])
    w1a_b, w3a_b, w2a_b, w_mk_b, w_mv_b = _cast_bf16([mat(a) for a in (w1a, w3a, w2a, w_mk, w_mv)])
    mk, mv, mk_t, mv_b = _mem_kv(mem_prompt, row(g_mem[l]), w_mk_b, w_mv_b)
    hp, w_in_b, w_pa_b, w_pb_b, w_o_b, w_q_b, w_co_b, w_pool_b = _ffn(
        x_prompt.reshape(nb * seq, D_MODEL), row(g_ff1[l]), w1a_b, w3a_b, w2a_b,
        side_cast=[mat(a) for a in (w_in, w_pa, w_pb, w_o, w_q, w_co, w_pool)], name="ffn1_prompt")
    w = dict(
        g_mix=row(g_mix[l]), w_in=w_in_b, b_gate=row(b_gate[l]),
        w_pool=w_pool_b.reshape(POOL_GROUPS, POOL_GW, POOL_GW),
        pool_scale=row(pool_scale[l]), g_sgu=row(g_sgu[l]), w_s=w_s[l], b_s_t=b_s[l].T,
        w_pa=w_pa_b, w_pb=w_pb_b, w_o=w_o_b, g_ca=row(g_ca[l]), w_q=w_q_b, w_co=w_co_b,
        w_s_x=jnp.repeat(jnp.transpose(w_s[l][:, :t, :t], (1, 2, 0)), SGU_HW, axis=-1),
        b_s_x=jnp.repeat(b_s[l][:, :t].T, SGU_HW, axis=-1),
    )

    hp, pool_p, w1b_b, w3b_b, w2b_b = _mixer_prompt(hp.reshape(nb, seq, D_MODEL), mk_t, mv_b, w,
                                                    side_cast=[mat(a) for a in (w1b, w3b, w2b)])
    y_prompt = _ffn(hp.reshape(nb * seq, D_MODEL), row(g_ff2[l]), w1b_b, w3b_b, w2b_b,
                    g_final=row(g_final), name="ffn2_prompt").reshape(nb, seq, D_MODEL)

    xs = jnp.transpose(x_sample, (1, 0, 2)).reshape(t * nbs, D_MODEL)
    st = jnp.transpose(state_pool[l], (1, 0, 2)).reshape(POOL_STATE * nbs, D_POOL)
    hs = _ffn(xs, row(g_ff1[l]), w1a_b, w3a_b, w2a_b, name="ffn1_sample")
    hs, qs, pool_s, vn_s = _mixer_sample(hs, st, w, nb=nbs, t=t, past_len=PAST_LEN)
    to_stream_major = lambda a, n: jnp.transpose(a.reshape(n, nbs, a.shape[-1]), (1, 0, 2))
    hs = to_stream_major(hs, t)
    qs = to_stream_major(qs, t)
    os_ = _attn_sample(qs, cache_mem_k, cache_mem_v)
    y_sample = _ffn(hs.reshape(nbs * t, D_MODEL), row(g_ff2[l]), w1b_b, w3b_b, w2b_b,
                    o=os_.reshape(nbs * t, D_MODEL), wco=w["w_co"], g_final=row(g_final),
                    name="ffn2_sample").reshape(nbs, t, D_MODEL)

    pool_prompt = pool_p[:, POOL_HALO - POOL_STATE:, :][None]
    pool_sample = to_stream_major(pool_s, POOL_STATE)[None]
    sgu_v_sample = to_stream_major(vn_s, t)[None]
    return (y_prompt, y_sample, pool_prompt, pool_sample, sgu_v_sample, mk, mv)
```

```python
import functools

import jax
import jax.numpy as jnp
from jax import lax
from jax.experimental import pallas as pl
from jax.experimental.pallas import tpu as pltpu

D_MODEL = 1024
PAST_LEN = 1024
CHUNK = 64
N_MEM = 256
MEM_HEADS = 4
MEM_HEAD_DIM = D_MODEL // MEM_HEADS
D_POOL = D_MODEL // 2
POOL_WINDOWS = (2, 4, 8, 16)
POOL_GROUPS = len(POOL_WINDOWS)
POOL_GW = D_POOL // POOL_GROUPS
POOL_STATE = max(POOL_WINDOWS) - 1
D_SGU = D_MODEL // 2
SGU_HEADS = 4
SGU_HW = D_SGU // SGU_HEADS
SGU_CHUNK = 128
D_FF = 2816
D_IN = D_POOL + 2 * D_SGU + 2 * D_MODEL
EPS = 1e-6

V7X_VMEM_LIMIT_BYTES = 56 * 1024 * 1024
F32_SUBLANES = 8
BF16_SUBLANES = 16
LANES = 128
CAST_STEPS = 16
POOL_HALO = 2 * F32_SUBLANES
assert POOL_HALO >= POOL_STATE + 1

FFN_ROWS = 1024
FFN_SUB_ROWS = 256
FFN_SIDE_BY_SIDE = 2
MIX_ROWS = 512
MIX_SUB_ROWS = 256
ATTN_SAMPLE_STREAMS = 4

BF16 = jnp.bfloat16
F32 = jnp.float32


def _dot(a, b):
    return jnp.dot(a, b, preferred_element_type=F32)


def _rmsnorm(x, g):
    y = x * lax.rsqrt(jnp.mean(x * x, axis=-1, keepdims=True) + EPS)
    return y * g


def _layernorm(x, g):
    mu = jnp.mean(x, axis=-1, keepdims=True)
    xc = x - mu
    y = xc * lax.rsqrt(jnp.mean(xc * xc, axis=-1, keepdims=True) + EPS)
    return y * g


def _const_spec(shape):
    nd = len(shape)
    return pl.BlockSpec(shape, lambda *_: (0,) * nd, pipeline_mode=pl.Buffered(1))


def _params(*sem):
    return pltpu.CompilerParams(dimension_semantics=sem, vmem_limit_bytes=V7X_VMEM_LIMIT_BYTES)


def _cast_kernel(*refs):
    n = len(refs) // 2
    for src, dst in zip(refs[:n], refs[n:]):
        dst[...] = src[...].astype(BF16)


def _side_cast_specs(arrays, steps, flat_step):
    specs = []
    for a in arrays:
        rows = a.shape[0]
        n = next(n for n in range(steps, 0, -1) if rows % (n * BF16_SUBLANES) == 0)
        specs.append(pl.BlockSpec((rows // n, a.shape[1]),
                                  lambda *g, n=n: (jnp.minimum(flat_step(*g), n - 1), 0)))
    return specs


def _cast_bf16(arrays):
    for a in arrays:
        assert a.ndim == 2 and a.shape[0] % (CAST_STEPS * BF16_SUBLANES) == 0 and a.shape[1] % LANES == 0
    specs = [pl.BlockSpec((a.shape[0] // CAST_STEPS, a.shape[1]), lambda i: (i, 0)) for a in arrays]
    return pl.pallas_call(
        _cast_kernel,
        out_shape=tuple(jax.ShapeDtypeStruct(a.shape, BF16) for a in arrays),
        grid=(CAST_STEPS,),
        in_specs=specs,
        out_specs=tuple(specs),
        compiler_params=_params("arbitrary"),
        name="cast_weights",
    )(*arrays)


def _mem_kv_kernel(mem_ref, g_ref, wk_ref, wv_ref, wq_ref, wco_ref, k_ref, v_ref, qk_ref, vo_ref):
    mn = _rmsnorm(mem_ref[0], g_ref[...]).astype(BF16)
    k = _dot(mn, wk_ref[...])
    v = _dot(mn, wv_ref[...])
    kt = k.T.astype(BF16)
    vb = v.astype(BF16)
    for hd in range(MEM_HEADS):
        sl = slice(hd * MEM_HEAD_DIM, (hd + 1) * MEM_HEAD_DIM)
        mem = slice(hd * N_MEM, (hd + 1) * N_MEM)
        k_ref[0, 0, :, hd, :] = k[:, sl]
        v_ref[0, 0, :, hd, :] = v[:, sl]
        qk_ref[0, :, mem] = (_dot(wq_ref[:, sl], kt[sl, :]) * (MEM_HEAD_DIM ** -0.5)).astype(BF16)
        vo_ref[0, mem, :] = _dot(vb[:, sl], wco_ref[sl, :]).astype(BF16)


def _mem_kv(mem, g, wk, wv, wq, wco):
    nb = mem.shape[0]
    blk = pl.BlockSpec((1, N_MEM, D_MODEL), lambda b: (b, 0, 0))
    blk_qk = pl.BlockSpec((1, D_MODEL, MEM_HEADS * N_MEM), lambda b: (b, 0, 0))
    blk_vo = pl.BlockSpec((1, MEM_HEADS * N_MEM, D_MODEL), lambda b: (b, 0, 0))
    blk_heads = pl.BlockSpec((1, 1, N_MEM, MEM_HEADS, MEM_HEAD_DIM), lambda b: (0, b, 0, 0, 0))
    square = _const_spec((D_MODEL, D_MODEL))
    return pl.pallas_call(
        _mem_kv_kernel,
        out_shape=(jax.ShapeDtypeStruct((1, nb, N_MEM, MEM_HEADS, MEM_HEAD_DIM), F32),) * 2
        + (jax.ShapeDtypeStruct((nb, D_MODEL, MEM_HEADS * N_MEM), BF16),
           jax.ShapeDtypeStruct((nb, MEM_HEADS * N_MEM, D_MODEL), BF16)),
        grid=(nb,),
        in_specs=[blk, _const_spec((1, D_MODEL)), square, square, square, square],
        out_specs=(blk_heads, blk_heads, blk_qk, blk_vo),
        compiler_params=_params("arbitrary"),
        name="mem_kv",
    )(mem, g, wk, wv, wq, wco)


def _pool_fold_kernel(wpool_ref, pscale_ref, wpa_ref, out_ref):
    for g in range(POOL_GROUPS):
        sl = slice(g * POOL_GW, (g + 1) * POOL_GW)
        scaled = (wpool_ref[g] * pscale_ref[:, sl]).astype(BF16)
        out_ref[sl, :] = _dot(scaled, wpa_ref[sl, :].astype(BF16)).astype(BF16)


def _pool_fold(w_pool, pool_scale, w_pa):
    args = [w_pool, pool_scale, w_pa]
    return pl.pallas_call(
        _pool_fold_kernel,
        out_shape=jax.ShapeDtypeStruct((D_POOL, D_MODEL), BF16),
        grid=(1,),
        in_specs=[_const_spec(a.shape) for a in args],
        out_specs=_const_spec((D_POOL, D_MODEL)),
        compiler_params=_params("arbitrary"),
        name="pool_fold",
    )(*args)


def _run_side_by_side(stage_generators):
    live = list(stage_generators)
    while live:
        live = [g for g in live if next(g, StopIteration) is not StopIteration]


def _ffn_kernel(*refs, pre, final, sub, n_cast):
    refs = list(refs)
    x_ref = refs.pop(0)
    if pre:
        o_ref_in, wco_ref = refs.pop(0), refs.pop(0)
    g_ref, w1_ref, w3_ref, w2_ref = refs[:4]
    refs = refs[4:]
    if final:
        gf_ref = refs.pop(0)
    cast_in, out_ref, cast_out = refs[:n_cast], refs[n_cast], refs[n_cast + 1:]
    _cast_kernel(*cast_in, *cast_out)

    def sub_tile(r0):
        rows = slice(r0, r0 + sub)
        x = x_ref[rows, :]
        if pre:
            x = x + _dot(o_ref_in[rows, :], wco_ref[...])
        n = _rmsnorm(x, g_ref[...]).astype(BF16)
        yield
        a = _dot(n, w1_ref[...])
        yield
        b = _dot(n, w3_ref[...])
        yield
        mid = (jax.nn.silu(a) * b).astype(BF16)
        y = x + 0.5 * _dot(mid, w2_ref[...])
        if final:
            y = _rmsnorm(y, gf_ref[...])
        out_ref[rows, :] = y

    starts = list(range(0, x_ref.shape[0], sub))
    for i in range(0, len(starts), FFN_SIDE_BY_SIDE):
        _run_side_by_side([sub_tile(r0) for r0 in starts[i:i + FFN_SIDE_BY_SIDE]])


def _ffn(x, g, w1, w3, w2, *, o=None, wco=None, g_final=None, side_cast=(), name):
    rows = x.shape[0]
    tm = min(FFN_ROWS, rows)
    assert rows % tm == 0
    pre, final = o is not None, g_final is not None
    row_spec = pl.BlockSpec((tm, D_MODEL), lambda i: (i, 0))
    args, specs = [x], [row_spec]
    if pre:
        args += [o, wco]
        specs += [row_spec, _const_spec((D_MODEL, D_MODEL))]
    args += [g, w1, w3, w2]
    specs += [_const_spec((1, D_MODEL)), _const_spec((D_MODEL, D_FF)), _const_spec((D_MODEL, D_FF)),
              _const_spec((D_FF, D_MODEL))]
    if final:
        args.append(g_final)
        specs.append(_const_spec((1, D_MODEL)))
    cast_specs = _side_cast_specs(side_cast, rows // tm, lambda i: i)
    outs = pl.pallas_call(
        functools.partial(_ffn_kernel, pre=pre, final=final, sub=min(FFN_SUB_ROWS, tm), n_cast=len(side_cast)),
        out_shape=(jax.ShapeDtypeStruct((rows, D_MODEL), F32),)
        + tuple(jax.ShapeDtypeStruct(a.shape, BF16) for a in side_cast),
        grid=(rows // tm,),
        in_specs=specs + cast_specs,
        out_specs=(row_spec,) + tuple(cast_specs),
        compiler_params=_params("arbitrary"),
        name=name,
    )(*args, *side_cast)
    return outs if side_cast else outs[0]


def _in_proj(h, g_mix, w_in, b_gate):
    n = _rmsnorm(h, g_mix).astype(BF16)
    z = _dot(n, w_in)
    xa = z[:, :D_POOL]
    uv = jax.nn.gelu(z[:, D_POOL:D_POOL + 2 * D_SGU])
    gate = jax.nn.sigmoid(z[:, D_POOL + 2 * D_SGU:] + b_gate)
    return xa, uv[:, :D_SGU], uv[:, D_SGU:], gate[:, :D_MODEL], gate[:, D_MODEL:]


def _pool_project(pooled, xa, wpool_ref, pool_scale):
    mixed = []
    for g in range(POOL_GROUPS):
        sl = slice(g * POOL_GW, (g + 1) * POOL_GW)
        d = (pooled[g] - xa[:, sl]).astype(BF16)
        mixed.append(_dot(d, wpool_ref[g]))
    return jnp.concatenate(mixed, axis=-1) * pool_scale


def _merge(h, a, us, g_a, g_b, wpa_ref, wpb_ref, wo_ref):
    merged = g_a * _dot(a.astype(BF16), wpa_ref[...]) + g_b * _dot(us.astype(BF16), wpb_ref[...])
    return h + _dot(merged.astype(BF16), wo_ref[...])


def _softmax(sc):
    e = jnp.exp(sc - jnp.max(sc, axis=-1, keepdims=True))
    return e / jnp.sum(e, axis=-1, keepdims=True)


def _window_sums(halo, xa_g, window):
    s = jnp.concatenate([halo, xa_g], axis=0)
    step = 1
    while step < window:
        s = s + pltpu.roll(s, step, axis=0)
        step *= 2
    return s[POOL_HALO:]


def _mixer_prompt_kernel(h_ref, qk_ref, vo_ref, gmix_ref, win_ref, bgate_ref, wpoolpa_ref,
                         gsgu_ref, ws_ref, bst_ref, wpb_ref, wo_ref, gca_ref,
                         *rest, tm, sub, n_cast):
    cast_in, (out_ref, pool_ref), cast_out = rest[:n_cast], rest[n_cast:n_cast + 2], rest[n_cast + 2:-1]
    carry_ref = rest[-1]
    _cast_kernel(*cast_in, *cast_out)
    j = pl.program_id(1)

    @pl.when(j == 0)
    def _():
        carry_ref[...] = jnp.zeros((POOL_HALO, D_POOL), F32)

    blk_r = lax.broadcasted_iota(jnp.int32, (SGU_CHUNK, SGU_CHUNK), 0) // CHUNK
    blk_c = lax.broadcasted_iota(jnp.int32, (SGU_CHUNK, SGU_CHUNK), 1) // CHUNK
    wm = [jnp.where(blk_r >= blk_c, ws_ref[hd], 0.0).astype(BF16) for hd in range(SGU_HEADS)]
    n_chunks = sub // SGU_CHUNK

    n_sub = tm // sub
    halos = [carry_ref[...]] + [None] * n_sub

    def sub_tile(i):
        r0 = i * sub
        h = h_ref[0, r0:r0 + sub, :]
        xa, u, v, g_a, g_b = _in_proj(h, gmix_ref[...], win_ref[...], bgate_ref[...])
        halos[i + 1] = xa[sub - POOL_HALO:, :]
        yield

        pos = j * tm + r0 + lax.broadcasted_iota(jnp.int32, (sub, 1), 0)
        d = []
        for g, w in enumerate(POOL_WINDOWS):
            sl = slice(g * POOL_GW, (g + 1) * POOL_GW)
            cnt = jnp.minimum(w, pos + 1).astype(F32)
            d.append(_window_sums(halos[i][:, sl], xa[:, sl], w) / cnt - xa[:, sl])
        a_pa = _dot(jnp.concatenate(d, axis=-1).astype(BF16), wpoolpa_ref[...])
        yield

        vn = _layernorm(v, gsgu_ref[...]).astype(BF16)
        s_heads = []
        for hd in range(SGU_HEADS):
            cols = slice(hd * SGU_HW, (hd + 1) * SGU_HW)
            vcat = jnp.concatenate([vn[c * SGU_CHUNK:(c + 1) * SGU_CHUNK, cols] for c in range(n_chunks)],
                                   axis=-1)
            sh = _dot(wm[hd], vcat) + bst_ref[:, hd:hd + 1]
            s_heads.append(jnp.concatenate([sh[:, c * SGU_HW:(c + 1) * SGU_HW] for c in range(n_chunks)],
                                           axis=0))
        s = jnp.concatenate(s_heads, axis=-1)
        yield

        merged = g_a * a_pa + g_b * _dot((u * s).astype(BF16), wpb_ref[...])
        h = h + _dot(merged.astype(BF16), wo_ref[...])
        yield
        sc = _dot(_rmsnorm(h, gca_ref[...]).astype(BF16), qk_ref[0])
        yield
        p = jnp.concatenate([_softmax(sc[:, hd * N_MEM:(hd + 1) * N_MEM]) for hd in range(MEM_HEADS)],
                            axis=-1).astype(BF16)
        yield
        out_ref[0, r0:r0 + sub, :] = h + _dot(p, vo_ref[0])

    _run_side_by_side([sub_tile(i) for i in range(n_sub)])

    carry_ref[...] = halos[n_sub]

    @pl.when(j == pl.num_programs(1) - 1)
    def _():
        pool_ref[0] = halos[n_sub]


def _mixer_prompt(h, qk, vo, w, side_cast=()):
    nb, seq, _ = h.shape
    tm, sub = MIX_ROWS, MIX_SUB_ROWS
    assert seq % tm == 0 and tm % sub == 0 and sub % SGU_CHUNK == 0
    row_spec = pl.BlockSpec((1, tm, D_MODEL), lambda b, j: (b, j, 0))
    kt_spec = pl.BlockSpec((1, D_MODEL, MEM_HEADS * N_MEM), lambda b, j: (b, 0, 0))
    v_spec = pl.BlockSpec((1, MEM_HEADS * N_MEM, D_MODEL), lambda b, j: (b, 0, 0))
    consts = [w["g_mix"], w["w_in"], w["b_gate"], w["w_pool_pa"], w["g_sgu"], w["w_s"],
              w["b_s_t"], w["w_pb"], w["w_o"], w["g_ca"]]
    tiles = seq // tm
    cast_specs = _side_cast_specs(side_cast, nb * tiles, lambda b, j: b * tiles + j)
    return pl.pallas_call(
        functools.partial(_mixer_prompt_kernel, tm=tm, sub=sub, n_cast=len(side_cast)),
        out_shape=(jax.ShapeDtypeStruct((nb, seq, D_MODEL), F32),
                   jax.ShapeDtypeStruct((nb, POOL_HALO, D_POOL), F32))
        + tuple(jax.ShapeDtypeStruct(a.shape, BF16) for a in side_cast),
        grid=(nb, tiles),
        in_specs=[row_spec, kt_spec, v_spec] + [_const_spec(c.shape) for c in consts] + cast_specs,
        out_specs=(row_spec, pl.BlockSpec((1, POOL_HALO, D_POOL), lambda b, j: (b, 0, 0))) + tuple(cast_specs),
        scratch_shapes=[pltpu.VMEM((POOL_HALO, D_POOL), F32)],
        compiler_params=_params("arbitrary", "arbitrary"),
        name="mixer_prompt",
    )(h, qk, vo, *consts, *side_cast)


def _mixer_sample_kernel(h_ref, state_ref, gmix_ref, win_ref, bgate_ref, wpool_ref, pscale_ref,
                         gsgu_ref, wsx_ref, bsx_ref, wpa_ref, wpb_ref, wo_ref, gca_ref, wq_ref,
                         out_ref, q_ref, pool_ref, vn_ref, *, nb, t, past_len):
    rows = t * nb
    halo = POOL_STATE * nb
    h = h_ref[...]
    xa, u, v, g_a, g_b = _in_proj(h, gmix_ref[...], win_ref[...], bgate_ref[...])

    cat = jnp.concatenate([state_ref[...], xa], axis=0)
    pooled = []
    for g, w in enumerate(POOL_WINDOWS):
        sl = slice(g * POOL_GW, (g + 1) * POOL_GW)
        acc = xa[:, sl]
        for back in range(1, w):
            acc = acc + cat[halo - back * nb:halo - back * nb + rows, sl]
        pos = past_len + lax.broadcasted_iota(jnp.int32, (rows, 1), 0) // nb
        cnt = jnp.minimum(w, pos + 1).astype(F32)
        pooled.append(acc / cnt)
    a = _pool_project(pooled, xa, wpool_ref, pscale_ref[...])
    pool_ref[...] = cat[rows:rows + halo, :]

    vn = _layernorm(v, gsgu_ref[...])
    vn_ref[...] = vn
    s_rows = []
    for p in range(t):
        acc = jnp.broadcast_to(bsx_ref[p:p + 1, :], (nb, D_SGU))
        for qq in range(t):
            if p // CHUNK >= qq // CHUNK:
                acc = acc + wsx_ref[p, qq:qq + 1, :] * vn[qq * nb:(qq + 1) * nb, :]
        s_rows.append(acc)
    s = jnp.concatenate(s_rows, axis=0)

    h = _merge(h, a, u * s, g_a, g_b, wpa_ref, wpb_ref, wo_ref)
    out_ref[...] = h
    q_ref[...] = _dot(_rmsnorm(h, gca_ref[...]).astype(BF16), wq_ref[...]).astype(BF16)


def _mixer_sample(h, state, w, *, nb, t, past_len):
    rows = nb * t
    consts = [w["g_mix"], w["w_in"], w["b_gate"], w["w_pool"], w["pool_scale"], w["g_sgu"], w["w_s_x"],
              w["b_s_x"], w["w_pa"], w["w_pb"], w["w_o"], w["g_ca"], w["w_q"]]
    args = [h, state] + consts
    return pl.pallas_call(
        functools.partial(_mixer_sample_kernel, nb=nb, t=t, past_len=past_len),
        out_shape=(jax.ShapeDtypeStruct((rows, D_MODEL), F32),
                   jax.ShapeDtypeStruct((rows, D_MODEL), BF16),
                   jax.ShapeDtypeStruct((POOL_STATE * nb, D_POOL), F32),
                   jax.ShapeDtypeStruct((rows, D_SGU), F32)),
        grid=(1,),
        in_specs=[_const_spec(a.shape) for a in args],
        out_specs=(_const_spec((rows, D_MODEL)), _const_spec((rows, D_MODEL)),
                   _const_spec((POOL_STATE * nb, D_POOL)), _const_spec((rows, D_SGU))),
        compiler_params=_params("arbitrary"),
        name="mixer_sample",
    )(*args)


def _attn_sample_kernel(q_ref, k_ref, v_ref, o_ref, *, streams):
    t = q_ref.shape[1]
    n_cols = N_MEM * MEM_HEADS
    row_head = lax.broadcasted_iota(jnp.int32, (MEM_HEADS * t, n_cols), 0) // t
    col_head = lax.broadcasted_iota(jnp.int32, (MEM_HEADS * t, n_cols), 1) % MEM_HEADS
    own_head = row_head == col_head

    def stream(s):
        q = q_ref[s]
        q_rows = jnp.concatenate([q[:, hd * MEM_HEAD_DIM:(hd + 1) * MEM_HEAD_DIM] for hd in range(MEM_HEADS)],
                                 axis=0)
        k = k_ref[0, s].reshape(n_cols, MEM_HEAD_DIM).astype(BF16)
        v = v_ref[0, s].reshape(n_cols, MEM_HEAD_DIM).astype(BF16)
        yield
        sc = lax.dot_general(q_rows, k, (((1,), (1,)), ((), ())), preferred_element_type=F32)
        sc = jnp.where(own_head, sc * (MEM_HEAD_DIM ** -0.5), -jnp.inf)
        yield
        p = _softmax(sc)
        yield
        o = _dot(p.astype(BF16), v)
        o_ref[s] = jnp.concatenate([o[hd * t:(hd + 1) * t, :] for hd in range(MEM_HEADS)],
                                   axis=-1).astype(BF16)

    _run_side_by_side([stream(s) for s in range(streams)])


def _attn_sample(q, k, v):
    nb, t, _ = q.shape
    streams = ATTN_SAMPLE_STREAMS
    assert nb % streams == 0
    q_spec = pl.BlockSpec((streams, t, D_MODEL), lambda b: (b, 0, 0))
    kv_spec = pl.BlockSpec((1, streams, N_MEM, MEM_HEADS, MEM_HEAD_DIM), lambda b: (0, b, 0, 0, 0))
    return pl.pallas_call(
        functools.partial(_attn_sample_kernel, streams=streams),
        out_shape=jax.ShapeDtypeStruct((nb, t, D_MODEL), BF16),
        grid=(nb // streams,),
        in_specs=[q_spec, kv_spec, kv_spec],
        out_specs=q_spec,
        compiler_params=_params("arbitrary"),
        name="attn_sample",
    )(q, k, v)


def kernel(x_prompt, x_sample, state_pool, cache_mem_k, cache_mem_v, mem_prompt, g_ff1, w1a, w3a, w2a,
           g_mix, w_in, b_gate, w_pool, pool_scale, g_sgu, w_s, b_s, w_pa, w_pb, w_o, g_mem, w_mk, w_mv,
           g_ca, w_q, w_co, g_ff2, w1b, w3b, w2b, g_final):
    nb, seq, _ = x_prompt.shape
    nbs, t, _ = x_sample.shape
    depth = g_ff1.shape[0]
    assert depth == 1
    l = 0
    row = lambda a: a.reshape(1, -1)
    mat = lambda a: a[l].reshape(-1, a.shape[-1])
    w1a_b, w3a_b, w2a_b = _cast_bf16([mat(a) for a in (w1a, w3a, w2a)])
    hp, w_in_b, w_pa_b, w_pb_b, w_o_b, w_q_b, w_co_b, w_pool_b, w_mk_b, w_mv_b = _ffn(
        x_prompt.reshape(nb * seq, D_MODEL), row(g_ff1[l]), w1a_b, w3a_b, w2a_b,
        side_cast=[mat(a) for a in (w_in, w_pa, w_pb, w_o, w_q, w_co, w_pool, w_mk, w_mv)], name="ffn1_prompt")
    mk, mv, qk, vo = _mem_kv(mem_prompt, row(g_mem[l]), w_mk_b, w_mv_b, w_q_b, w_co_b)
    w = dict(
        g_mix=row(g_mix[l]), w_in=w_in_b, b_gate=row(b_gate[l]),
        w_pool=w_pool_b.reshape(POOL_GROUPS, POOL_GW, POOL_GW),
        w_pool_pa=_pool_fold(w_pool[l], row(pool_scale[l]), w_pa[l]),
        pool_scale=row(pool_scale[l]), g_sgu=row(g_sgu[l]), w_s=w_s[l], b_s_t=b_s[l].T,
        w_pa=w_pa_b, w_pb=w_pb_b, w_o=w_o_b, g_ca=row(g_ca[l]), w_q=w_q_b, w_co=w_co_b,
        w_s_x=jnp.repeat(jnp.transpose(w_s[l][:, :t, :t], (1, 2, 0)), SGU_HW, axis=-1),
        b_s_x=jnp.repeat(b_s[l][:, :t].T, SGU_HW, axis=-1),
    )

    hp, pool_p, w1b_b, w3b_b, w2b_b = _mixer_prompt(hp.reshape(nb, seq, D_MODEL), qk, vo, w,
                                                    side_cast=[mat(a) for a in (w1b, w3b, w2b)])
    y_prompt = _ffn(hp.reshape(nb * seq, D_MODEL), row(g_ff2[l]), w1b_b, w3b_b, w2b_b,
                    g_final=row(g_final), name="ffn2_prompt").reshape(nb, seq, D_MODEL)

    xs = jnp.transpose(x_sample, (1, 0, 2)).reshape(t * nbs, D_MODEL)
    st = jnp.transpose(state_pool[l], (1, 0, 2)).reshape(POOL_STATE * nbs, D_POOL)
    hs = _ffn(xs, row(g_ff1[l]), w1a_b, w3a_b, w2a_b, name="ffn1_sample")
    hs, qs, pool_s, vn_s = _mixer_sample(hs, st, w, nb=nbs, t=t, past_len=PAST_LEN)
    to_stream_major = lambda a, n: jnp.transpose(a.reshape(n, nbs, a.shape[-1]), (1, 0, 2))
    hs = to_stream_major(hs, t)
    qs = to_stream_major(qs, t)
    os_ = _attn_sample(qs, cache_mem_k, cache_mem_v)
    y_sample = _ffn(hs.reshape(nbs * t, D_MODEL), row(g_ff2[l]), w1b_b, w3b_b, w2b_b,
                    o=os_.reshape(nbs * t, D_MODEL), wco=w["w_co"], g_final=row(g_final),
                    name="ffn2_sample").reshape(nbs, t, D_MODEL)

    pool_prompt = pool_p[:, POOL_HALO - POOL_STATE:, :][None]
    pool_sample = to_stream_major(pool_s, POOL_STATE)[None]
    sgu_v_sample = to_stream_major(vn_s, t)[None]
    return (y_prompt, y_sample, pool_prompt, pool_sample, sgu_v_sample, mk, mv)
```

```python
import functools

import jax
import jax.numpy as jnp
from jax import lax
from jax.experimental import pallas as pl
from jax.experimental.pallas import tpu as pltpu

D_MODEL = 1024
PAST_LEN = 1024
CHUNK = 64
N_MEM = 256
MEM_HEADS = 4
MEM_HEAD_DIM = D_MODEL // MEM_HEADS
D_POOL = D_MODEL // 2
POOL_WINDOWS = (2, 4, 8, 16)
POOL_GROUPS = len(POOL_WINDOWS)
POOL_GW = D_POOL // POOL_GROUPS
POOL_STATE = max(POOL_WINDOWS) - 1
D_SGU = D_MODEL // 2
SGU_HEADS = 4
SGU_HW = D_SGU // SGU_HEADS
SGU_CHUNK = 128
D_FF = 2816
D_IN = D_POOL + 2 * D_SGU + 2 * D_MODEL
EPS = 1e-6

V7X_VMEM_LIMIT_BYTES = 56 * 1024 * 1024
F32_SUBLANES = 8
BF16_SUBLANES = 16
LANES = 128
CAST_STEPS = 16
POOL_HALO = 2 * F32_SUBLANES
assert POOL_HALO >= POOL_STATE + 1

FFN_ROWS = 1024
FFN_SUB_ROWS = 256
FFN_SIDE_BY_SIDE = 2
MIX_ROWS = 512
MIX_SUB_ROWS = 256
ATTN_SAMPLE_STREAMS = 4

BF16 = jnp.bfloat16
F32 = jnp.float32


def _dot(a, b):
    return jnp.dot(a, b, preferred_element_type=F32)


def _rmsnorm(x, g):
    y = x * lax.rsqrt(jnp.mean(x * x, axis=-1, keepdims=True) + EPS)
    return y * g


def _layernorm(x, g):
    mu = jnp.mean(x, axis=-1, keepdims=True)
    xc = x - mu
    y = xc * lax.rsqrt(jnp.mean(xc * xc, axis=-1, keepdims=True) + EPS)
    return y * g


def _const_spec(shape):
    nd = len(shape)
    return pl.BlockSpec(shape, lambda *_: (0,) * nd, pipeline_mode=pl.Buffered(1))


def _params(*sem):
    return pltpu.CompilerParams(dimension_semantics=sem, vmem_limit_bytes=V7X_VMEM_LIMIT_BYTES)


def _cast_kernel(*refs):
    n = len(refs) // 2
    for src, dst in zip(refs[:n], refs[n:]):
        dst[...] = src[...].astype(BF16)


def _side_cast_specs(arrays, steps, flat_step):
    specs = []
    for a in arrays:
        rows = a.shape[0]
        n = next(n for n in range(steps, 0, -1) if rows % (n * BF16_SUBLANES) == 0)
        specs.append(pl.BlockSpec((rows // n, a.shape[1]),
                                  lambda *g, n=n: (jnp.minimum(flat_step(*g), n - 1), 0)))
    return specs


def _cast_bf16(arrays):
    for a in arrays:
        assert a.ndim == 2 and a.shape[0] % (CAST_STEPS * BF16_SUBLANES) == 0 and a.shape[1] % LANES == 0
    specs = [pl.BlockSpec((a.shape[0] // CAST_STEPS, a.shape[1]), lambda i: (i, 0)) for a in arrays]
    return pl.pallas_call(
        _cast_kernel,
        out_shape=tuple(jax.ShapeDtypeStruct(a.shape, BF16) for a in arrays),
        grid=(CAST_STEPS,),
        in_specs=specs,
        out_specs=tuple(specs),
        compiler_params=_params("arbitrary"),
        name="cast_weights",
    )(*arrays)


def _mem_kv_kernel(mem_ref, g_ref, wk_ref, wv_ref, wq_ref, wco_ref, k_ref, v_ref, qk_ref, vo_ref):
    mn = _rmsnorm(mem_ref[0], g_ref[...]).astype(BF16)
    k = _dot(mn, wk_ref[...])
    v = _dot(mn, wv_ref[...])
    kt = k.T.astype(BF16)
    vb = v.astype(BF16)
    for hd in range(MEM_HEADS):
        sl = slice(hd * MEM_HEAD_DIM, (hd + 1) * MEM_HEAD_DIM)
        mem = slice(hd * N_MEM, (hd + 1) * N_MEM)
        k_ref[0, 0, :, hd, :] = k[:, sl]
        v_ref[0, 0, :, hd, :] = v[:, sl]
        qk_ref[0, :, mem] = (_dot(wq_ref[:, sl], kt[sl, :]) * (MEM_HEAD_DIM ** -0.5)).astype(BF16)
        vo_ref[0, mem, :] = _dot(vb[:, sl], wco_ref[sl, :]).astype(BF16)


def _mem_kv(mem, g, wk, wv, wq, wco):
    nb = mem.shape[0]
    blk = pl.BlockSpec((1, N_MEM, D_MODEL), lambda b: (b, 0, 0))
    blk_qk = pl.BlockSpec((1, D_MODEL, MEM_HEADS * N_MEM), lambda b: (b, 0, 0))
    blk_vo = pl.BlockSpec((1, MEM_HEADS * N_MEM, D_MODEL), lambda b: (b, 0, 0))
    blk_heads = pl.BlockSpec((1, 1, N_MEM, MEM_HEADS, MEM_HEAD_DIM), lambda b: (0, b, 0, 0, 0))
    square = _const_spec((D_MODEL, D_MODEL))
    return pl.pallas_call(
        _mem_kv_kernel,
        out_shape=(jax.ShapeDtypeStruct((1, nb, N_MEM, MEM_HEADS, MEM_HEAD_DIM), F32),) * 2
        + (jax.ShapeDtypeStruct((nb, D_MODEL, MEM_HEADS * N_MEM), BF16),
           jax.ShapeDtypeStruct((nb, MEM_HEADS * N_MEM, D_MODEL), BF16)),
        grid=(nb,),
        in_specs=[blk, _const_spec((1, D_MODEL)), square, square, square, square],
        out_specs=(blk_heads, blk_heads, blk_qk, blk_vo),
        compiler_params=_params("arbitrary"),
        name="mem_kv",
    )(mem, g, wk, wv, wq, wco)


def _pool_fold_kernel(wpool_ref, pscale_ref, wpa_ref, out_ref):
    for g in range(POOL_GROUPS):
        sl = slice(g * POOL_GW, (g + 1) * POOL_GW)
        scaled = (wpool_ref[g] * pscale_ref[:, sl]).astype(BF16)
        out_ref[sl, :] = _dot(scaled, wpa_ref[sl, :].astype(BF16)).astype(BF16)


def _pool_fold(w_pool, pool_scale, w_pa):
    args = [w_pool, pool_scale, w_pa]
    return pl.pallas_call(
        _pool_fold_kernel,
        out_shape=jax.ShapeDtypeStruct((D_POOL, D_MODEL), BF16),
        grid=(1,),
        in_specs=[_const_spec(a.shape) for a in args],
        out_specs=_const_spec((D_POOL, D_MODEL)),
        compiler_params=_params("arbitrary"),
        name="pool_fold",
    )(*args)


def _run_side_by_side(stage_generators):
    live = list(stage_generators)
    while live:
        live = [g for g in live if next(g, StopIteration) is not StopIteration]


def _ffn_kernel(*refs, final, sub, n_cast, tail, tail_pre):
    refs = list(refs)
    x_ref, g_ref, w1_ref, w3_ref, w2_ref = refs[:5]
    refs = refs[5:]
    gf_ref = refs.pop(0) if final else None
    tail_x_ref = refs.pop(0) if tail else None
    tail_o_ref, wco_ref = (refs.pop(0), refs.pop(0)) if tail_pre else (None, None)
    cast_in, refs = refs[:n_cast], refs[n_cast:]
    out_ref = refs.pop(0)
    tail_out_ref = refs.pop(0) if tail else None
    cast_out = refs
    _cast_kernel(*cast_in, *cast_out)

    def sub_tile(x_ref, o_ref, out_ref, r0):
        rows = slice(r0, r0 + sub)
        x = x_ref[rows, :]
        if o_ref is not None:
            x = x + _dot(o_ref[rows, :], wco_ref[...])
        n = _rmsnorm(x, g_ref[...]).astype(BF16)
        yield
        a = _dot(n, w1_ref[...])
        yield
        b = _dot(n, w3_ref[...])
        yield
        mid = (jax.nn.silu(a) * b).astype(BF16)
        y = x + 0.5 * _dot(mid, w2_ref[...])
        if final:
            y = _rmsnorm(y, gf_ref[...])
        out_ref[rows, :] = y

    def tile(x_ref, o_ref, out_ref):
        starts = list(range(0, x_ref.shape[0], sub))
        for i in range(0, len(starts), FFN_SIDE_BY_SIDE):
            _run_side_by_side([sub_tile(x_ref, o_ref, out_ref, r0) for r0 in starts[i:i + FFN_SIDE_BY_SIDE]])

    tile(x_ref, None, out_ref)

    if tail:
        @pl.when(pl.program_id(0) == pl.num_programs(0) - 1)
        def _():
            tile(tail_x_ref, tail_o_ref, tail_out_ref)


def _ffn(x, g, w1, w3, w2, *, g_final=None, tail_x=None, tail_o=None, wco=None, side_cast=(), name):
    rows = x.shape[0]
    tm = min(FFN_ROWS, rows)
    sub = min(FFN_SUB_ROWS, tm)
    assert rows % tm == 0 and tm % sub == 0
    final, tail, tail_pre = g_final is not None, tail_x is not None, tail_o is not None
    row_spec = pl.BlockSpec((tm, D_MODEL), lambda i: (i, 0))
    args = [x, g, w1, w3, w2]
    specs = [row_spec, _const_spec((1, D_MODEL)), _const_spec((D_MODEL, D_FF)), _const_spec((D_MODEL, D_FF)),
             _const_spec((D_FF, D_MODEL))]
    if final:
        args.append(g_final)
        specs.append(_const_spec((1, D_MODEL)))
    out_shapes, out_specs = [jax.ShapeDtypeStruct((rows, D_MODEL), F32)], [row_spec]
    if tail:
        assert tail_x.shape[0] % sub == 0
        args.append(tail_x)
        specs.append(_const_spec(tail_x.shape))
        out_shapes.append(jax.ShapeDtypeStruct(tail_x.shape, F32))
        out_specs.append(_const_spec(tail_x.shape))
    if tail_pre:
        args += [tail_o, wco]
        specs += [_const_spec(tail_o.shape), _const_spec((D_MODEL, D_MODEL))]
    cast_specs = _side_cast_specs(side_cast, rows // tm, lambda i: i)
    outs = pl.pallas_call(
        functools.partial(_ffn_kernel, final=final, sub=sub, n_cast=len(side_cast), tail=tail, tail_pre=tail_pre),
        out_shape=tuple(out_shapes) + tuple(jax.ShapeDtypeStruct(a.shape, BF16) for a in side_cast),
        grid=(rows // tm,),
        in_specs=specs + cast_specs,
        out_specs=tuple(out_specs) + tuple(cast_specs),
        compiler_params=_params("arbitrary"),
        name=name,
    )(*args, *side_cast)
    n_main = len(out_shapes)
    return outs[0], (outs[1] if tail else None), tuple(outs[n_main:])


def _in_proj(h, g_mix, w_in, b_gate):
    n = _rmsnorm(h, g_mix).astype(BF16)
    z = _dot(n, w_in)
    xa = z[:, :D_POOL]
    uv = jax.nn.gelu(z[:, D_POOL:D_POOL + 2 * D_SGU])
    gate = jax.nn.sigmoid(z[:, D_POOL + 2 * D_SGU:] + b_gate)
    return xa, uv[:, :D_SGU], uv[:, D_SGU:], gate[:, :D_MODEL], gate[:, D_MODEL:]


def _pool_project(pooled, xa, wpool_ref, pool_scale):
    mixed = []
    for g in range(POOL_GROUPS):
        sl = slice(g * POOL_GW, (g + 1) * POOL_GW)
        d = (pooled[g] - xa[:, sl]).astype(BF16)
        mixed.append(_dot(d, wpool_ref[g]))
    return jnp.concatenate(mixed, axis=-1) * pool_scale


def _merge(h, a, us, g_a, g_b, wpa_ref, wpb_ref, wo_ref):
    merged = g_a * _dot(a.astype(BF16), wpa_ref[...]) + g_b * _dot(us.astype(BF16), wpb_ref[...])
    return h + _dot(merged.astype(BF16), wo_ref[...])


def _softmax(sc):
    e = jnp.exp(sc - jnp.max(sc, axis=-1, keepdims=True))
    return e / jnp.sum(e, axis=-1, keepdims=True)


def _window_sums(halo, xa_g, window):
    s = jnp.concatenate([halo, xa_g], axis=0)
    step = 1
    while step < window:
        s = s + pltpu.roll(s, step, axis=0)
        step *= 2
    return s[POOL_HALO:]


def _mixer_prompt_kernel(h_ref, qk_ref, vo_ref, gmix_ref, win_ref, bgate_ref, wpoolpa_ref,
                         gsgu_ref, ws_ref, bst_ref, wpb_ref, wo_ref, gca_ref,
                         *rest, tm, sub, n_cast):
    cast_in, (out_ref, pool_ref), cast_out = rest[:n_cast], rest[n_cast:n_cast + 2], rest[n_cast + 2:-1]
    carry_ref = rest[-1]
    _cast_kernel(*cast_in, *cast_out)
    j = pl.program_id(1)

    @pl.when(j == 0)
    def _():
        carry_ref[...] = jnp.zeros((POOL_HALO, D_POOL), F32)

    blk_r = lax.broadcasted_iota(jnp.int32, (SGU_CHUNK, SGU_CHUNK), 0) // CHUNK
    blk_c = lax.broadcasted_iota(jnp.int32, (SGU_CHUNK, SGU_CHUNK), 1) // CHUNK
    wm = [jnp.where(blk_r >= blk_c, ws_ref[hd], 0.0).astype(BF16) for hd in range(SGU_HEADS)]
    n_chunks = sub // SGU_CHUNK

    n_sub = tm // sub
    halos = [carry_ref[...]] + [None] * n_sub

    def sub_tile(i):
        r0 = i * sub
        h = h_ref[0, r0:r0 + sub, :]
        xa, u, v, g_a, g_b = _in_proj(h, gmix_ref[...], win_ref[...], bgate_ref[...])
        halos[i + 1] = xa[sub - POOL_HALO:, :]
        yield

        pos = j * tm + r0 + lax.broadcasted_iota(jnp.int32, (sub, 1), 0)
        d = []
        for g, w in enumerate(POOL_WINDOWS):
            sl = slice(g * POOL_GW, (g + 1) * POOL_GW)
            cnt = jnp.minimum(w, pos + 1).astype(F32)
            d.append(_window_sums(halos[i][:, sl], xa[:, sl], w) / cnt - xa[:, sl])
        a_pa = _dot(jnp.concatenate(d, axis=-1).astype(BF16), wpoolpa_ref[...])
        yield

        vn = _layernorm(v, gsgu_ref[...]).astype(BF16)
        s_heads = []
        for hd in range(SGU_HEADS):
            cols = slice(hd * SGU_HW, (hd + 1) * SGU_HW)
            vcat = jnp.concatenate([vn[c * SGU_CHUNK:(c + 1) * SGU_CHUNK, cols] for c in range(n_chunks)],
                                   axis=-1)
            sh = _dot(wm[hd], vcat) + bst_ref[:, hd:hd + 1]
            s_heads.append(jnp.concatenate([sh[:, c * SGU_HW:(c + 1) * SGU_HW] for c in range(n_chunks)],
                                           axis=0))
        s = jnp.concatenate(s_heads, axis=-1)
        yield

        merged = g_a * a_pa + g_b * _dot((u * s).astype(BF16), wpb_ref[...])
        h = h + _dot(merged.astype(BF16), wo_ref[...])
        yield
        sc = _dot(_rmsnorm(h, gca_ref[...]).astype(BF16), qk_ref[0])
        yield
        p = jnp.concatenate([_softmax(sc[:, hd * N_MEM:(hd + 1) * N_MEM]) for hd in range(MEM_HEADS)],
                            axis=-1).astype(BF16)
        yield
        out_ref[0, r0:r0 + sub, :] = h + _dot(p, vo_ref[0])

    _run_side_by_side([sub_tile(i) for i in range(n_sub)])

    carry_ref[...] = halos[n_sub]

    @pl.when(j == pl.num_programs(1) - 1)
    def _():
        pool_ref[0] = halos[n_sub]


def _mixer_prompt(h, qk, vo, w, side_cast=()):
    nb, seq, _ = h.shape
    tm, sub = MIX_ROWS, MIX_SUB_ROWS
    assert seq % tm == 0 and tm % sub == 0 and sub % SGU_CHUNK == 0
    row_spec = pl.BlockSpec((1, tm, D_MODEL), lambda b, j: (b, j, 0))
    kt_spec = pl.BlockSpec((1, D_MODEL, MEM_HEADS * N_MEM), lambda b, j: (b, 0, 0))
    v_spec = pl.BlockSpec((1, MEM_HEADS * N_MEM, D_MODEL), lambda b, j: (b, 0, 0))
    consts = [w["g_mix"], w["w_in"], w["b_gate"], w["w_pool_pa"], w["g_sgu"], w["w_s"],
              w["b_s_t"], w["w_pb"], w["w_o"], w["g_ca"]]
    tiles = seq // tm
    cast_specs = _side_cast_specs(side_cast, nb * tiles, lambda b, j: b * tiles + j)
    return pl.pallas_call(
        functools.partial(_mixer_prompt_kernel, tm=tm, sub=sub, n_cast=len(side_cast)),
        out_shape=(jax.ShapeDtypeStruct((nb, seq, D_MODEL), F32),
                   jax.ShapeDtypeStruct((nb, POOL_HALO, D_POOL), F32))
        + tuple(jax.ShapeDtypeStruct(a.shape, BF16) for a in side_cast),
        grid=(nb, tiles),
        in_specs=[row_spec, kt_spec, v_spec] + [_const_spec(c.shape) for c in consts] + cast_specs,
        out_specs=(row_spec, pl.BlockSpec((1, POOL_HALO, D_POOL), lambda b, j: (b, 0, 0))) + tuple(cast_specs),
        scratch_shapes=[pltpu.VMEM((POOL_HALO, D_POOL), F32)],
        compiler_params=_params("arbitrary", "arbitrary"),
        name="mixer_prompt",
    )(h, qk, vo, *consts, *side_cast)


def _mixer_sample_kernel(h_ref, state_ref, gmix_ref, win_ref, bgate_ref, wpool_ref, pscale_ref,
                         gsgu_ref, wsx_ref, bsx_ref, wpa_ref, wpb_ref, wo_ref, gca_ref, wq_ref,
                         out_ref, q_ref, pool_ref, vn_ref, *, nb, t, past_len):
    rows = t * nb
    halo = POOL_STATE * nb
    h = h_ref[...]
    xa, u, v, g_a, g_b = _in_proj(h, gmix_ref[...], win_ref[...], bgate_ref[...])

    cat = jnp.concatenate([state_ref[...], xa], axis=0)
    pooled = []
    for g, w in enumerate(POOL_WINDOWS):
        sl = slice(g * POOL_GW, (g + 1) * POOL_GW)
        acc = xa[:, sl]
        for back in range(1, w):
            acc = acc + cat[halo - back * nb:halo - back * nb + rows, sl]
        pos = past_len + lax.broadcasted_iota(jnp.int32, (rows, 1), 0) // nb
        cnt = jnp.minimum(w, pos + 1).astype(F32)
        pooled.append(acc / cnt)
    a = _pool_project(pooled, xa, wpool_ref, pscale_ref[...])
    pool_ref[...] = cat[rows:rows + halo, :]

    vn = _layernorm(v, gsgu_ref[...])
    vn_ref[...] = vn
    s_rows = []
    for p in range(t):
        acc = jnp.broadcast_to(bsx_ref[p:p + 1, :], (nb, D_SGU))
        for qq in range(t):
            if p // CHUNK >= qq // CHUNK:
                acc = acc + wsx_ref[p, qq:qq + 1, :] * vn[qq * nb:(qq + 1) * nb, :]
        s_rows.append(acc)
    s = jnp.concatenate(s_rows, axis=0)

    h = _merge(h, a, u * s, g_a, g_b, wpa_ref, wpb_ref, wo_ref)
    out_ref[...] = h
    q_ref[...] = _dot(_rmsnorm(h, gca_ref[...]).astype(BF16), wq_ref[...]).astype(BF16)


def _mixer_sample(h, state, w, *, nb, t, past_len):
    rows = nb * t
    consts = [w["g_mix"], w["w_in"], w["b_gate"], w["w_pool"], w["pool_scale"], w["g_sgu"], w["w_s_x"],
              w["b_s_x"], w["w_pa"], w["w_pb"], w["w_o"], w["g_ca"], w["w_q"]]
    args = [h, state] + consts
    return pl.pallas_call(
        functools.partial(_mixer_sample_kernel, nb=nb, t=t, past_len=past_len),
        out_shape=(jax.ShapeDtypeStruct((rows, D_MODEL), F32),
                   jax.ShapeDtypeStruct((rows, D_MODEL), BF16),
                   jax.ShapeDtypeStruct((POOL_STATE * nb, D_POOL), F32),
                   jax.ShapeDtypeStruct((rows, D_SGU), F32)),
        grid=(1,),
        in_specs=[_const_spec(a.shape) for a in args],
        out_specs=(_const_spec((rows, D_MODEL)), _const_spec((rows, D_MODEL)),
                   _const_spec((POOL_STATE * nb, D_POOL)), _const_spec((rows, D_SGU))),
        compiler_params=_params("arbitrary"),
        name="mixer_sample",
    )(*args)


def _attn_sample_kernel(q_ref, k_ref, v_ref, o_ref, *, streams):
    t = q_ref.shape[1]
    n_cols = N_MEM * MEM_HEADS
    row_head = lax.broadcasted_iota(jnp.int32, (MEM_HEADS * t, n_cols), 0) // t
    col_head = lax.broadcasted_iota(jnp.int32, (MEM_HEADS * t, n_cols), 1) % MEM_HEADS
    own_head = row_head == col_head

    def stream(s):
        q = q_ref[s]
        q_rows = jnp.concatenate([q[:, hd * MEM_HEAD_DIM:(hd + 1) * MEM_HEAD_DIM] for hd in range(MEM_HEADS)],
                                 axis=0)
        k = k_ref[0, s].reshape(n_cols, MEM_HEAD_DIM).astype(BF16)
        v = v_ref[0, s].reshape(n_cols, MEM_HEAD_DIM).astype(BF16)
        yield
        sc = lax.dot_general(q_rows, k, (((1,), (1,)), ((), ())), preferred_element_type=F32)
        sc = jnp.where(own_head, sc * (MEM_HEAD_DIM ** -0.5), -jnp.inf)
        yield
        p = _softmax(sc)
        yield
        o = _dot(p.astype(BF16), v)
        o_ref[s] = jnp.concatenate([o[hd * t:(hd + 1) * t, :] for hd in range(MEM_HEADS)],
                                   axis=-1).astype(BF16)

    _run_side_by_side([stream(s) for s in range(streams)])


def _attn_sample(q, k, v):
    nb, t, _ = q.shape
    streams = ATTN_SAMPLE_STREAMS
    assert nb % streams == 0
    q_spec = pl.BlockSpec((streams, t, D_MODEL), lambda b: (b, 0, 0))
    kv_spec = pl.BlockSpec((1, streams, N_MEM, MEM_HEADS, MEM_HEAD_DIM), lambda b: (0, b, 0, 0, 0))
    return pl.pallas_call(
        functools.partial(_attn_sample_kernel, streams=streams),
        out_shape=jax.ShapeDtypeStruct((nb, t, D_MODEL), BF16),
        grid=(nb // streams,),
        in_specs=[q_spec, kv_spec, kv_spec],
        out_specs=q_spec,
        compiler_params=_params("arbitrary"),
        name="attn_sample",
    )(q, k, v)


def kernel(x_prompt, x_sample, state_pool, cache_mem_k, cache_mem_v, mem_prompt, g_ff1, w1a, w3a, w2a,
           g_mix, w_in, b_gate, w_pool, pool_scale, g_sgu, w_s, b_s, w_pa, w_pb, w_o, g_mem, w_mk, w_mv,
           g_ca, w_q, w_co, g_ff2, w1b, w3b, w2b, g_final):
    nb, seq, _ = x_prompt.shape
    nbs, t, _ = x_sample.shape
    depth = g_ff1.shape[0]
    assert depth == 1
    l = 0
    row = lambda a: a.reshape(1, -1)
    mat = lambda a: a[l].reshape(-1, a.shape[-1])
    w1a_b, w3a_b, w2a_b = _cast_bf16([mat(a) for a in (w1a, w3a, w2a)])

    xs = jnp.transpose(x_sample, (1, 0, 2)).reshape(t * nbs, D_MODEL)
    st = jnp.transpose(state_pool[l], (1, 0, 2)).reshape(POOL_STATE * nbs, D_POOL)
    to_stream_major = lambda a, n: jnp.transpose(a.reshape(n, nbs, a.shape[-1]), (1, 0, 2))

    hp, hs, (w_in_b, w_pa_b, w_pb_b, w_o_b, w_q_b, w_co_b, w_pool_b, w_mk_b, w_mv_b) = _ffn(
        x_prompt.reshape(nb * seq, D_MODEL), row(g_ff1[l]), w1a_b, w3a_b, w2a_b, tail_x=xs,
        side_cast=[mat(a) for a in (w_in, w_pa, w_pb, w_o, w_q, w_co, w_pool, w_mk, w_mv)], name="ffn1")
    mk, mv, qk, vo = _mem_kv(mem_prompt, row(g_mem[l]), w_mk_b, w_mv_b, w_q_b, w_co_b)
    w = dict(
        g_mix=row(g_mix[l]), w_in=w_in_b, b_gate=row(b_gate[l]),
        w_pool=w_pool_b.reshape(POOL_GROUPS, POOL_GW, POOL_GW),
        w_pool_pa=_pool_fold(w_pool[l], row(pool_scale[l]), w_pa[l]),
        pool_scale=row(pool_scale[l]), g_sgu=row(g_sgu[l]), w_s=w_s[l], b_s_t=b_s[l].T,
        w_pa=w_pa_b, w_pb=w_pb_b, w_o=w_o_b, g_ca=row(g_ca[l]), w_q=w_q_b,
        w_s_x=jnp.repeat(jnp.transpose(w_s[l][:, :t, :t], (1, 2, 0)), SGU_HW, axis=-1),
        b_s_x=jnp.repeat(b_s[l][:, :t].T, SGU_HW, axis=-1),
    )

    hs, qs, pool_s, vn_s = _mixer_sample(hs, st, w, nb=nbs, t=t, past_len=PAST_LEN)
    hs = to_stream_major(hs, t).reshape(nbs * t, D_MODEL)
    os_ = _attn_sample(to_stream_major(qs, t), cache_mem_k, cache_mem_v).reshape(nbs * t, D_MODEL)
    hp, pool_p, w1b_b, w3b_b, w2b_b = _mixer_prompt(hp.reshape(nb, seq, D_MODEL), qk, vo, w,
                                                    side_cast=[mat(a) for a in (w1b, w3b, w2b)])

    y_prompt, y_sample, _ = _ffn(hp.reshape(nb * seq, D_MODEL), row(g_ff2[l]), w1b_b, w3b_b, w2b_b,
                                 g_final=row(g_final), tail_x=hs, tail_o=os_, wco=w_co_b, name="ffn2")

    pool_prompt = pool_p[:, POOL_HALO - POOL_STATE:, :][None]
    pool_sample = to_stream_major(pool_s, POOL_STATE)[None]
    sgu_v_sample = to_stream_major(vn_s, t)[None]
    return (y_prompt.reshape(nb, seq, D_MODEL), y_sample.reshape(nbs, t, D_MODEL), pool_prompt, pool_sample,
            sgu_v_sample, mk, mv)
```

```python
import functools

import jax
import jax.numpy as jnp
from jax import lax
from jax.experimental import pallas as pl
from jax.experimental.pallas import tpu as pltpu

D_MODEL = 1024
PAST_LEN = 1024
CHUNK = 64
N_MEM = 256
MEM_HEADS = 4
MEM_HEAD_DIM = D_MODEL // MEM_HEADS
D_POOL = D_MODEL // 2
POOL_WINDOWS = (2, 4, 8, 16)
POOL_GROUPS = len(POOL_WINDOWS)
POOL_GW = D_POOL // POOL_GROUPS
POOL_STATE = max(POOL_WINDOWS) - 1
D_SGU = D_MODEL // 2
SGU_HEADS = 4
SGU_HW = D_SGU // SGU_HEADS
SGU_CHUNK = 128
D_FF = 2816
D_IN = D_POOL + 2 * D_SGU + 2 * D_MODEL
EPS = 1e-6

V7X_VMEM_LIMIT_BYTES = 56 * 1024 * 1024
F32_SUBLANES = 8
BF16_SUBLANES = 16
LANES = 128
CAST_STEPS = 16
POOL_HALO = 2 * F32_SUBLANES
assert POOL_HALO >= POOL_STATE + 1

FFN_ROWS = 1024
FFN_SUB_ROWS = 256
FFN_SIDE_BY_SIDE = 2
MIX_ROWS = 512
MIX_SUB_ROWS = 256
ATTN_SAMPLE_STREAMS = 4

BF16 = jnp.bfloat16
F32 = jnp.float32


def _dot(a, b):
    return jnp.dot(a, b, preferred_element_type=F32)


def _rmsnorm(x, g):
    y = x * lax.rsqrt(jnp.mean(x * x, axis=-1, keepdims=True) + EPS)
    return y * g


def _layernorm(x, g):
    mu = jnp.mean(x, axis=-1, keepdims=True)
    xc = x - mu
    y = xc * lax.rsqrt(jnp.mean(xc * xc, axis=-1, keepdims=True) + EPS)
    return y * g


def _const_spec(shape):
    nd = len(shape)
    return pl.BlockSpec(shape, lambda *_: (0,) * nd, pipeline_mode=pl.Buffered(1))


def _params(*sem):
    return pltpu.CompilerParams(dimension_semantics=sem, vmem_limit_bytes=V7X_VMEM_LIMIT_BYTES)


def _cast_kernel(*refs):
    n = len(refs) // 2
    for src, dst in zip(refs[:n], refs[n:]):
        dst[...] = src[...].astype(BF16)


def _side_cast_specs(arrays, steps, flat_step):
    specs = []
    for a in arrays:
        rows = a.shape[0]
        n = next(n for n in range(steps, 0, -1) if rows % (n * BF16_SUBLANES) == 0)
        specs.append(pl.BlockSpec((rows // n, a.shape[1]),
                                  lambda *g, n=n: (jnp.minimum(flat_step(*g), n - 1), 0)))
    return specs


def _cast_bf16(arrays):
    for a in arrays:
        assert a.ndim == 2 and a.shape[0] % (CAST_STEPS * BF16_SUBLANES) == 0 and a.shape[1] % LANES == 0
    specs = [pl.BlockSpec((a.shape[0] // CAST_STEPS, a.shape[1]), lambda i: (i, 0)) for a in arrays]
    return pl.pallas_call(
        _cast_kernel,
        out_shape=tuple(jax.ShapeDtypeStruct(a.shape, BF16) for a in arrays),
        grid=(CAST_STEPS,),
        in_specs=specs,
        out_specs=tuple(specs),
        compiler_params=_params("arbitrary"),
        name="cast_weights",
    )(*arrays)


def _mem_kv_kernel(mem_ref, g_ref, wk_ref, wv_ref, wq_ref, wco_ref, k_ref, v_ref, qk_ref, vo_ref):
    mn = _rmsnorm(mem_ref[0], g_ref[...]).astype(BF16)
    k = _dot(mn, wk_ref[...])
    v = _dot(mn, wv_ref[...])
    kt = k.T.astype(BF16)
    vb = v.astype(BF16)
    for hd in range(MEM_HEADS):
        sl = slice(hd * MEM_HEAD_DIM, (hd + 1) * MEM_HEAD_DIM)
        mem = slice(hd * N_MEM, (hd + 1) * N_MEM)
        k_ref[0, 0, :, hd, :] = k[:, sl]
        v_ref[0, 0, :, hd, :] = v[:, sl]
        qk_ref[0, :, mem] = (_dot(wq_ref[:, sl], kt[sl, :]) * (MEM_HEAD_DIM ** -0.5)).astype(BF16)
        vo_ref[0, mem, :] = _dot(vb[:, sl], wco_ref[sl, :]).astype(BF16)


def _mem_kv(mem, g, wk, wv, wq, wco):
    nb = mem.shape[0]
    blk = pl.BlockSpec((1, N_MEM, D_MODEL), lambda b: (b, 0, 0))
    blk_qk = pl.BlockSpec((1, D_MODEL, MEM_HEADS * N_MEM), lambda b: (b, 0, 0))
    blk_vo = pl.BlockSpec((1, MEM_HEADS * N_MEM, D_MODEL), lambda b: (b, 0, 0))
    blk_heads = pl.BlockSpec((1, 1, N_MEM, MEM_HEADS, MEM_HEAD_DIM), lambda b: (0, b, 0, 0, 0))
    square = _const_spec((D_MODEL, D_MODEL))
    return pl.pallas_call(
        _mem_kv_kernel,
        out_shape=(jax.ShapeDtypeStruct((1, nb, N_MEM, MEM_HEADS, MEM_HEAD_DIM), F32),) * 2
        + (jax.ShapeDtypeStruct((nb, D_MODEL, MEM_HEADS * N_MEM), BF16),
           jax.ShapeDtypeStruct((nb, MEM_HEADS * N_MEM, D_MODEL), BF16)),
        grid=(nb,),
        in_specs=[blk, _const_spec((1, D_MODEL)), square, square, square, square],
        out_specs=(blk_heads, blk_heads, blk_qk, blk_vo),
        compiler_params=_params("arbitrary"),
        name="mem_kv",
    )(mem, g, wk, wv, wq, wco)


def _pool_fold_kernel(wpool_ref, pscale_ref, wpa_ref, out_ref):
    for g in range(POOL_GROUPS):
        sl = slice(g * POOL_GW, (g + 1) * POOL_GW)
        scaled = (wpool_ref[g] * pscale_ref[:, sl]).astype(BF16)
        out_ref[sl, :] = _dot(scaled, wpa_ref[sl, :].astype(BF16)).astype(BF16)


def _pool_fold(w_pool, pool_scale, w_pa):
    args = [w_pool, pool_scale, w_pa]
    return pl.pallas_call(
        _pool_fold_kernel,
        out_shape=jax.ShapeDtypeStruct((D_POOL, D_MODEL), BF16),
        grid=(1,),
        in_specs=[_const_spec(a.shape) for a in args],
        out_specs=_const_spec((D_POOL, D_MODEL)),
        compiler_params=_params("arbitrary"),
        name="pool_fold",
    )(*args)


def _run_side_by_side(stage_generators):
    live = list(stage_generators)
    while live:
        live = [g for g in live if next(g, StopIteration) is not StopIteration]


def _ffn_kernel(*refs, final, sub, n_cast, tail, tail_pre):
    refs = list(refs)
    x_ref, g_ref, w1_ref, w3_ref, w2_ref = refs[:5]
    refs = refs[5:]
    gf_ref = refs.pop(0) if final else None
    tail_x_ref = refs.pop(0) if tail else None
    tail_o_ref, wco_ref = (refs.pop(0), refs.pop(0)) if tail_pre else (None, None)
    cast_in, refs = refs[:n_cast], refs[n_cast:]
    out_ref = refs.pop(0)
    tail_out_ref = refs.pop(0) if tail else None
    cast_out = refs
    _cast_kernel(*cast_in, *cast_out)

    def sub_tile(x_ref, o_ref, out_ref, r0):
        rows = slice(r0, r0 + sub)
        x = x_ref[rows, :]
        if o_ref is not None:
            x = x + _dot(o_ref[rows, :], wco_ref[...])
        n = _rmsnorm(x, g_ref[...]).astype(BF16)
        yield
        a = _dot(n, w1_ref[...])
        yield
        b = _dot(n, w3_ref[...])
        yield
        mid = (jax.nn.silu(a) * b).astype(BF16)
        y = x + 0.5 * _dot(mid, w2_ref[...])
        if final:
            y = _rmsnorm(y, gf_ref[...])
        out_ref[rows, :] = y

    def tile(x_ref, o_ref, out_ref):
        starts = list(range(0, x_ref.shape[0], sub))
        for i in range(0, len(starts), FFN_SIDE_BY_SIDE):
            _run_side_by_side([sub_tile(x_ref, o_ref, out_ref, r0) for r0 in starts[i:i + FFN_SIDE_BY_SIDE]])

    tile(x_ref, None, out_ref)

    if tail:
        @pl.when(pl.program_id(0) == pl.num_programs(0) - 1)
        def _():
            tile(tail_x_ref, tail_o_ref, tail_out_ref)


def _ffn(x, g, w1, w3, w2, *, g_final=None, tail_x=None, tail_o=None, wco=None, side_cast=(), name):
    rows = x.shape[0]
    tm = min(FFN_ROWS, rows)
    sub = min(FFN_SUB_ROWS, tm)
    assert rows % tm == 0 and tm % sub == 0
    final, tail, tail_pre = g_final is not None, tail_x is not None, tail_o is not None
    row_spec = pl.BlockSpec((tm, D_MODEL), lambda i: (i, 0))
    args = [x, g, w1, w3, w2]
    specs = [row_spec, _const_spec((1, D_MODEL)), _const_spec((D_MODEL, D_FF)), _const_spec((D_MODEL, D_FF)),
             _const_spec((D_FF, D_MODEL))]
    if final:
        args.append(g_final)
        specs.append(_const_spec((1, D_MODEL)))
    out_shapes, out_specs = [jax.ShapeDtypeStruct((rows, D_MODEL), F32)], [row_spec]
    if tail:
        assert tail_x.shape[0] % sub == 0
        args.append(tail_x)
        specs.append(_const_spec(tail_x.shape))
        out_shapes.append(jax.ShapeDtypeStruct(tail_x.shape, F32))
        out_specs.append(_const_spec(tail_x.shape))
    if tail_pre:
        args += [tail_o, wco]
        specs += [_const_spec(tail_o.shape), _const_spec((D_MODEL, D_MODEL))]
    cast_specs = _side_cast_specs(side_cast, rows // tm, lambda i: i)
    outs = pl.pallas_call(
        functools.partial(_ffn_kernel, final=final, sub=sub, n_cast=len(side_cast), tail=tail, tail_pre=tail_pre),
        out_shape=tuple(out_shapes) + tuple(jax.ShapeDtypeStruct(a.shape, BF16) for a in side_cast),
        grid=(rows // tm,),
        in_specs=specs + cast_specs,
        out_specs=tuple(out_specs) + tuple(cast_specs),
        compiler_params=_params("arbitrary"),
        name=name,
    )(*args, *side_cast)
    n_main = len(out_shapes)
    return outs[0], (outs[1] if tail else None), tuple(outs[n_main:])


def _in_proj(h, g_mix, w_in, b_gate):
    n = _rmsnorm(h, g_mix).astype(BF16)
    z = _dot(n, w_in)
    xa = z[:, :D_POOL]
    uv = jax.nn.gelu(z[:, D_POOL:D_POOL + 2 * D_SGU])
    gate = jax.nn.sigmoid(z[:, D_POOL + 2 * D_SGU:] + b_gate)
    return xa, uv[:, :D_SGU], uv[:, D_SGU:], gate[:, :D_MODEL], gate[:, D_MODEL:]


def _pool_project(pooled, xa, wpool_ref, pool_scale):
    mixed = []
    for g in range(POOL_GROUPS):
        sl = slice(g * POOL_GW, (g + 1) * POOL_GW)
        d = (pooled[g] - xa[:, sl]).astype(BF16)
        mixed.append(_dot(d, wpool_ref[g]))
    return jnp.concatenate(mixed, axis=-1) * pool_scale


def _merge(h, a, us, g_a, g_b, wpa_ref, wpb_ref, wo_ref):
    merged = g_a * _dot(a.astype(BF16), wpa_ref[...]) + g_b * _dot(us.astype(BF16), wpb_ref[...])
    return h + _dot(merged.astype(BF16), wo_ref[...])


def _softmax(sc):
    e = jnp.exp(sc - jnp.max(sc, axis=-1, keepdims=True))
    return e / jnp.sum(e, axis=-1, keepdims=True)


def _window_sums(halo, xa_g, window):
    s = jnp.concatenate([halo, xa_g], axis=0)
    step = 1
    while step < window:
        s = s + pltpu.roll(s, step, axis=0)
        step *= 2
    return s[POOL_HALO:]


def _mixer_prompt_kernel(h_ref, qk_ref, vo_ref, gmix_ref, win_ref, bgate_ref, wpoolpa_ref,
                         gsgu_ref, ws_ref, bst_ref, wpb_ref, wo_ref, gca_ref,
                         *rest, tm, sub, n_cast):
    cast_in, (out_ref, pool_ref), cast_out = rest[:n_cast], rest[n_cast:n_cast + 2], rest[n_cast + 2:-1]
    carry_ref = rest[-1]
    _cast_kernel(*cast_in, *cast_out)
    j = pl.program_id(1)

    @pl.when(j == 0)
    def _():
        carry_ref[...] = jnp.zeros((POOL_HALO, D_POOL), F32)

    blk_r = lax.broadcasted_iota(jnp.int32, (SGU_CHUNK, SGU_CHUNK), 0) // CHUNK
    blk_c = lax.broadcasted_iota(jnp.int32, (SGU_CHUNK, SGU_CHUNK), 1) // CHUNK
    wm = [jnp.where(blk_r >= blk_c, ws_ref[hd], 0.0).astype(BF16) for hd in range(SGU_HEADS)]
    n_chunks = sub // SGU_CHUNK

    n_sub = tm // sub
    halos = [carry_ref[...]] + [None] * n_sub

    def sub_tile(i):
        r0 = i * sub
        h = h_ref[0, r0:r0 + sub, :]
        xa, u, v, g_a, g_b = _in_proj(h, gmix_ref[...], win_ref[...], bgate_ref[...])
        halos[i + 1] = xa[sub - POOL_HALO:, :]
        yield

        pos = j * tm + r0 + lax.broadcasted_iota(jnp.int32, (sub, 1), 0)
        d = []
        for g, w in enumerate(POOL_WINDOWS):
            sl = slice(g * POOL_GW, (g + 1) * POOL_GW)
            cnt = jnp.minimum(w, pos + 1).astype(F32)
            d.append(_window_sums(halos[i][:, sl], xa[:, sl], w) / cnt - xa[:, sl])
        a_pa = _dot(jnp.concatenate(d, axis=-1).astype(BF16), wpoolpa_ref[...])
        yield

        vn = _layernorm(v, gsgu_ref[...]).astype(BF16)
        s_heads = []
        for hd in range(SGU_HEADS):
            cols = slice(hd * SGU_HW, (hd + 1) * SGU_HW)
            vcat = jnp.concatenate([vn[c * SGU_CHUNK:(c + 1) * SGU_CHUNK, cols] for c in range(n_chunks)],
                                   axis=-1)
            sh = _dot(wm[hd], vcat) + bst_ref[:, hd:hd + 1]
            s_heads.append(jnp.concatenate([sh[:, c * SGU_HW:(c + 1) * SGU_HW] for c in range(n_chunks)],
                                           axis=0))
        s = jnp.concatenate(s_heads, axis=-1)
        yield

        merged = g_a * a_pa + g_b * _dot((u * s).astype(BF16), wpb_ref[...])
        h = h + _dot(merged.astype(BF16), wo_ref[...])
        yield
        sc = _dot(_rmsnorm(h, gca_ref[...]).astype(BF16), qk_ref[0])
        yield
        p = jnp.concatenate([_softmax(sc[:, hd * N_MEM:(hd + 1) * N_MEM]) for hd in range(MEM_HEADS)],
                            axis=-1).astype(BF16)
        yield
        out_ref[0, r0:r0 + sub, :] = h + _dot(p, vo_ref[0])

    _run_side_by_side([sub_tile(i) for i in range(n_sub)])

    carry_ref[...] = halos[n_sub]

    @pl.when(j == pl.num_programs(1) - 1)
    def _():
        pool_ref[0] = halos[n_sub]


def _mixer_prompt(h, qk, vo, w, side_cast=()):
    nb, seq, _ = h.shape
    tm, sub = MIX_ROWS, MIX_SUB_ROWS
    assert seq % tm == 0 and tm % sub == 0 and sub % SGU_CHUNK == 0
    row_spec = pl.BlockSpec((1, tm, D_MODEL), lambda b, j: (b, j, 0))
    kt_spec = pl.BlockSpec((1, D_MODEL, MEM_HEADS * N_MEM), lambda b, j: (b, 0, 0))
    v_spec = pl.BlockSpec((1, MEM_HEADS * N_MEM, D_MODEL), lambda b, j: (b, 0, 0))
    consts = [w["g_mix"], w["w_in"], w["b_gate"], w["w_pool_pa"], w["g_sgu"], w["w_s"],
              w["b_s_t"], w["w_pb"], w["w_o"], w["g_ca"]]
    tiles = seq // tm
    cast_specs = _side_cast_specs(side_cast, nb * tiles, lambda b, j: b * tiles + j)
    return pl.pallas_call(
        functools.partial(_mixer_prompt_kernel, tm=tm, sub=sub, n_cast=len(side_cast)),
        out_shape=(jax.ShapeDtypeStruct((nb, seq, D_MODEL), F32),
                   jax.ShapeDtypeStruct((nb, POOL_HALO, D_POOL), F32))
        + tuple(jax.ShapeDtypeStruct(a.shape, BF16) for a in side_cast),
        grid=(nb, tiles),
        in_specs=[row_spec, kt_spec, v_spec] + [_const_spec(c.shape) for c in consts] + cast_specs,
        out_specs=(row_spec, pl.BlockSpec((1, POOL_HALO, D_POOL), lambda b, j: (b, 0, 0))) + tuple(cast_specs),
        scratch_shapes=[pltpu.VMEM((POOL_HALO, D_POOL), F32)],
        compiler_params=_params("arbitrary", "arbitrary"),
        name="mixer_prompt",
    )(h, qk, vo, *consts, *side_cast)


def _mixer_sample_kernel(h_ref, state_ref, gmix_ref, win_ref, bgate_ref, wpool_ref, pscale_ref,
                         gsgu_ref, wsx_ref, bsx_ref, wpa_ref, wpb_ref, wo_ref, gca_ref, wq_ref,
                         out_ref, q_ref, pool_ref, vn_ref, *, nb, t, past_len):
    rows = t * nb
    halo = POOL_STATE * nb

    def store_stream_major(ref, value):
        for i in range(t):
            ref[:, i, :] = value[i * nb:(i + 1) * nb, :]

    h = jnp.concatenate([h_ref[:, i, :] for i in range(t)], axis=0)
    xa, u, v, g_a, g_b = _in_proj(h, gmix_ref[...], win_ref[...], bgate_ref[...])

    state = [state_ref[:, i, :] for i in range(POOL_STATE)]
    cat = jnp.concatenate(state + [xa], axis=0)
    pooled = []
    for g, w in enumerate(POOL_WINDOWS):
        sl = slice(g * POOL_GW, (g + 1) * POOL_GW)
        acc = xa[:, sl]
        for back in range(1, w):
            acc = acc + cat[halo - back * nb:halo - back * nb + rows, sl]
        pos = past_len + lax.broadcasted_iota(jnp.int32, (rows, 1), 0) // nb
        cnt = jnp.minimum(w, pos + 1).astype(F32)
        pooled.append(acc / cnt)
    a = _pool_project(pooled, xa, wpool_ref, pscale_ref[...])
    for i in range(POOL_STATE):
        pool_ref[:, i, :] = cat[rows + i * nb:rows + (i + 1) * nb, :]

    vn = _layernorm(v, gsgu_ref[...])
    store_stream_major(vn_ref, vn)
    s_rows = []
    for p in range(t):
        acc = jnp.broadcast_to(bsx_ref[p:p + 1, :], (nb, D_SGU))
        for qq in range(t):
            if p // CHUNK >= qq // CHUNK:
                acc = acc + wsx_ref[p, qq:qq + 1, :] * vn[qq * nb:(qq + 1) * nb, :]
        s_rows.append(acc)
    s = jnp.concatenate(s_rows, axis=0)

    h = _merge(h, a, u * s, g_a, g_b, wpa_ref, wpb_ref, wo_ref)
    store_stream_major(out_ref, h)
    store_stream_major(q_ref, _dot(_rmsnorm(h, gca_ref[...]).astype(BF16), wq_ref[...]))


def _mixer_sample(h, state, w, *, nb, t, past_len):
    consts = [w["g_mix"], w["w_in"], w["b_gate"], w["w_pool"], w["pool_scale"], w["g_sgu"], w["w_s_x"],
              w["b_s_x"], w["w_pa"], w["w_pb"], w["w_o"], w["g_ca"], w["w_q"]]
    args = [h, state] + consts
    return pl.pallas_call(
        functools.partial(_mixer_sample_kernel, nb=nb, t=t, past_len=past_len),
        out_shape=(jax.ShapeDtypeStruct((nb, t, D_MODEL), F32),
                   jax.ShapeDtypeStruct((nb, t, D_MODEL), F32),
                   jax.ShapeDtypeStruct((nb, POOL_STATE, D_POOL), F32),
                   jax.ShapeDtypeStruct((nb, t, D_SGU), F32)),
        grid=(1,),
        in_specs=[_const_spec(a.shape) for a in args],
        out_specs=(_const_spec((nb, t, D_MODEL)), _const_spec((nb, t, D_MODEL)),
                   _const_spec((nb, POOL_STATE, D_POOL)), _const_spec((nb, t, D_SGU))),
        compiler_params=_params("arbitrary"),
        name="mixer_sample",
    )(*args)


def _attn_sample_kernel(q_ref, k_ref, v_ref, o_ref, *, streams):
    t = q_ref.shape[1]
    n_cols = N_MEM * MEM_HEADS
    row_head = lax.broadcasted_iota(jnp.int32, (MEM_HEADS * t, n_cols), 0) // t
    col_head = lax.broadcasted_iota(jnp.int32, (MEM_HEADS * t, n_cols), 1) % MEM_HEADS
    own_head = row_head == col_head

    def stream(s):
        q = q_ref[s].astype(BF16)
        q_rows = jnp.concatenate([q[:, hd * MEM_HEAD_DIM:(hd + 1) * MEM_HEAD_DIM] for hd in range(MEM_HEADS)],
                                 axis=0)
        k = k_ref[0, s].reshape(n_cols, MEM_HEAD_DIM).astype(BF16)
        v = v_ref[0, s].reshape(n_cols, MEM_HEAD_DIM).astype(BF16)
        yield
        sc = lax.dot_general(q_rows, k, (((1,), (1,)), ((), ())), preferred_element_type=F32)
        sc = jnp.where(own_head, sc * (MEM_HEAD_DIM ** -0.5), -jnp.inf)
        yield
        p = _softmax(sc)
        yield
        o = _dot(p.astype(BF16), v)
        o_ref[s] = jnp.concatenate([o[hd * t:(hd + 1) * t, :] for hd in range(MEM_HEADS)],
                                   axis=-1).astype(BF16)

    _run_side_by_side([stream(s) for s in range(streams)])


def _attn_sample(q, k, v):
    nb, t, _ = q.shape
    streams = ATTN_SAMPLE_STREAMS
    assert nb % streams == 0
    q_spec = pl.BlockSpec((streams, t, D_MODEL), lambda b: (b, 0, 0))
    kv_spec = pl.BlockSpec((1, streams, N_MEM, MEM_HEADS, MEM_HEAD_DIM), lambda b: (0, b, 0, 0, 0))
    return pl.pallas_call(
        functools.partial(_attn_sample_kernel, streams=streams),
        out_shape=jax.ShapeDtypeStruct((nb, t, D_MODEL), BF16),
        grid=(nb // streams,),
        in_specs=[q_spec, kv_spec, kv_spec],
        out_specs=q_spec,
        compiler_params=_params("arbitrary"),
        name="attn_sample",
    )(q, k, v)


def kernel(x_prompt, x_sample, state_pool, cache_mem_k, cache_mem_v, mem_prompt, g_ff1, w1a, w3a, w2a,
           g_mix, w_in, b_gate, w_pool, pool_scale, g_sgu, w_s, b_s, w_pa, w_pb, w_o, g_mem, w_mk, w_mv,
           g_ca, w_q, w_co, g_ff2, w1b, w3b, w2b, g_final):
    nb, seq, _ = x_prompt.shape
    nbs, t, _ = x_sample.shape
    depth = g_ff1.shape[0]
    assert depth == 1
    l = 0
    row = lambda a: a.reshape(1, -1)
    mat = lambda a: a[l].reshape(-1, a.shape[-1])
    w1a_b, w3a_b, w2a_b = _cast_bf16([mat(a) for a in (w1a, w3a, w2a)])

    xs = x_sample.reshape(nbs * t, D_MODEL)

    hp, hs, (w_in_b, w_pa_b, w_pb_b, w_o_b, w_q_b, w_co_b, w_pool_b, w_mk_b, w_mv_b) = _ffn(
        x_prompt.reshape(nb * seq, D_MODEL), row(g_ff1[l]), w1a_b, w3a_b, w2a_b, tail_x=xs,
        side_cast=[mat(a) for a in (w_in, w_pa, w_pb, w_o, w_q, w_co, w_pool, w_mk, w_mv)], name="ffn1")
    mk, mv, qk, vo = _mem_kv(mem_prompt, row(g_mem[l]), w_mk_b, w_mv_b, w_q_b, w_co_b)
    w = dict(
        g_mix=row(g_mix[l]), w_in=w_in_b, b_gate=row(b_gate[l]),
        w_pool=w_pool_b.reshape(POOL_GROUPS, POOL_GW, POOL_GW),
        w_pool_pa=_pool_fold(w_pool[l], row(pool_scale[l]), w_pa[l]),
        pool_scale=row(pool_scale[l]), g_sgu=row(g_sgu[l]), w_s=w_s[l], b_s_t=b_s[l].T,
        w_pa=w_pa_b, w_pb=w_pb_b, w_o=w_o_b, g_ca=row(g_ca[l]), w_q=w_q_b,
        w_s_x=jnp.repeat(jnp.transpose(w_s[l][:, :t, :t], (1, 2, 0)), SGU_HW, axis=-1),
        b_s_x=jnp.repeat(b_s[l][:, :t].T, SGU_HW, axis=-1),
    )

    hs, qs, pool_s, vn_s = _mixer_sample(hs.reshape(nbs, t, D_MODEL), state_pool[l], w, nb=nbs, t=t,
                                         past_len=PAST_LEN)
    hs = hs.reshape(nbs * t, D_MODEL)
    os_ = _attn_sample(qs, cache_mem_k, cache_mem_v).reshape(nbs * t, D_MODEL)
    hp, pool_p, w1b_b, w3b_b, w2b_b = _mixer_prompt(hp.reshape(nb, seq, D_MODEL), qk, vo, w,
                                                    side_cast=[mat(a) for a in (w1b, w3b, w2b)])

    y_prompt, y_sample, _ = _ffn(hp.reshape(nb * seq, D_MODEL), row(g_ff2[l]), w1b_b, w3b_b, w2b_b,
                                 g_final=row(g_final), tail_x=hs, tail_o=os_, wco=w_co_b, name="ffn2")

    pool_prompt = pool_p[:, POOL_HALO - POOL_STATE:, :][None]
    pool_sample = pool_s[None]
    sgu_v_sample = vn_s[None]
    return (y_prompt.reshape(nb, seq, D_MODEL), y_sample.reshape(nbs, t, D_MODEL), pool_prompt, pool_sample,
            sgu_v_sample, mk, mv)
```

```python
import functools

import jax
import jax.numpy as jnp
from jax import lax
from jax.experimental import pallas as pl
from jax.experimental.pallas import tpu as pltpu

D_MODEL = 1024
PAST_LEN = 1024
CHUNK = 64
N_MEM = 256
MEM_HEADS = 4
MEM_HEAD_DIM = D_MODEL // MEM_HEADS
D_POOL = D_MODEL // 2
POOL_WINDOWS = (2, 4, 8, 16)
POOL_GROUPS = len(POOL_WINDOWS)
POOL_GW = D_POOL // POOL_GROUPS
POOL_STATE = max(POOL_WINDOWS) - 1
D_SGU = D_MODEL // 2
SGU_HEADS = 4
SGU_HW = D_SGU // SGU_HEADS
SGU_CHUNK = 128
D_FF = 2816
D_IN = D_POOL + 2 * D_SGU + 2 * D_MODEL
EPS = 1e-6

V7X_VMEM_LIMIT_BYTES = 56 * 1024 * 1024
F32_SUBLANES = 8
BF16_SUBLANES = 16
LANES = 128
CAST_STEPS = 16
POOL_HALO = 2 * F32_SUBLANES
assert POOL_HALO >= POOL_STATE + 1

FFN_ROWS = 1024
FFN_SUB_ROWS = 256
FFN_SIDE_BY_SIDE = 2
MIX_ROWS = 512
MIX_SUB_ROWS = 256
ATTN_SAMPLE_STREAMS = 4

BF16 = jnp.bfloat16
F32 = jnp.float32


def _dot(a, b):
    return jnp.dot(a, b, preferred_element_type=F32)


def _rmsnorm(x, g):
    y = x * lax.rsqrt(jnp.mean(x * x, axis=-1, keepdims=True) + EPS)
    return y * g


def _layernorm(x, g):
    mu = jnp.mean(x, axis=-1, keepdims=True)
    xc = x - mu
    y = xc * lax.rsqrt(jnp.mean(xc * xc, axis=-1, keepdims=True) + EPS)
    return y * g


def _const_spec(shape):
    nd = len(shape)
    return pl.BlockSpec(shape, lambda *_: (0,) * nd, pipeline_mode=pl.Buffered(1))


def _params(*sem):
    return pltpu.CompilerParams(dimension_semantics=sem, vmem_limit_bytes=V7X_VMEM_LIMIT_BYTES)


def _cast_kernel(*refs):
    n = len(refs) // 2
    for src, dst in zip(refs[:n], refs[n:]):
        dst[...] = src[...].astype(BF16)


def _side_cast_specs(arrays, steps, flat_step):
    specs = []
    for a in arrays:
        rows = a.shape[0]
        n = next(n for n in range(steps, 0, -1) if rows % (n * BF16_SUBLANES) == 0)
        specs.append(pl.BlockSpec((rows // n, a.shape[1]),
                                  lambda *g, n=n: (jnp.minimum(flat_step(*g), n - 1), 0)))
    return specs


def _cast_bf16(arrays):
    for a in arrays:
        assert a.ndim == 2 and a.shape[0] % (CAST_STEPS * BF16_SUBLANES) == 0 and a.shape[1] % LANES == 0
    specs = [pl.BlockSpec((a.shape[0] // CAST_STEPS, a.shape[1]), lambda i: (i, 0)) for a in arrays]
    return pl.pallas_call(
        _cast_kernel,
        out_shape=tuple(jax.ShapeDtypeStruct(a.shape, BF16) for a in arrays),
        grid=(CAST_STEPS,),
        in_specs=specs,
        out_specs=tuple(specs),
        compiler_params=_params("arbitrary"),
        name="cast_weights",
    )(*arrays)


def _mem_kv_kernel(mem_ref, g_ref, wk_ref, wv_ref, wq_ref, wco_ref, k_ref, v_ref, qk_ref, vo_ref):
    mn = _rmsnorm(mem_ref[0], g_ref[...]).astype(BF16)
    k = _dot(mn, wk_ref[...])
    v = _dot(mn, wv_ref[...])
    kt = k.T.astype(BF16)
    vb = v.astype(BF16)
    for hd in range(MEM_HEADS):
        sl = slice(hd * MEM_HEAD_DIM, (hd + 1) * MEM_HEAD_DIM)
        mem = slice(hd * N_MEM, (hd + 1) * N_MEM)
        k_ref[0, 0, :, hd, :] = k[:, sl]
        v_ref[0, 0, :, hd, :] = v[:, sl]
        qk_ref[0, :, mem] = (_dot(wq_ref[:, sl], kt[sl, :]) * (MEM_HEAD_DIM ** -0.5)).astype(BF16)
        vo_ref[0, mem, :] = _dot(vb[:, sl], wco_ref[sl, :]).astype(BF16)


def _mem_kv(mem, g, wk, wv, wq, wco):
    nb = mem.shape[0]
    blk = pl.BlockSpec((1, N_MEM, D_MODEL), lambda b: (b, 0, 0))
    blk_qk = pl.BlockSpec((1, D_MODEL, MEM_HEADS * N_MEM), lambda b: (b, 0, 0))
    blk_vo = pl.BlockSpec((1, MEM_HEADS * N_MEM, D_MODEL), lambda b: (b, 0, 0))
    blk_heads = pl.BlockSpec((1, 1, N_MEM, MEM_HEADS, MEM_HEAD_DIM), lambda b: (0, b, 0, 0, 0))
    square = _const_spec((D_MODEL, D_MODEL))
    return pl.pallas_call(
        _mem_kv_kernel,
        out_shape=(jax.ShapeDtypeStruct((1, nb, N_MEM, MEM_HEADS, MEM_HEAD_DIM), F32),) * 2
        + (jax.ShapeDtypeStruct((nb, D_MODEL, MEM_HEADS * N_MEM), BF16),
           jax.ShapeDtypeStruct((nb, MEM_HEADS * N_MEM, D_MODEL), BF16)),
        grid=(nb,),
        in_specs=[blk, _const_spec((1, D_MODEL)), square, square, square, square],
        out_specs=(blk_heads, blk_heads, blk_qk, blk_vo),
        compiler_params=_params("arbitrary"),
        name="mem_kv",
    )(mem, g, wk, wv, wq, wco)


def _pool_fold_kernel(wpool_ref, pscale_ref, wpa_ref, out_ref):
    for g in range(POOL_GROUPS):
        sl = slice(g * POOL_GW, (g + 1) * POOL_GW)
        scaled = (wpool_ref[g] * pscale_ref[:, sl]).astype(BF16)
        out_ref[sl, :] = _dot(scaled, wpa_ref[sl, :].astype(BF16)).astype(BF16)


def _pool_fold(w_pool, pool_scale, w_pa):
    args = [w_pool, pool_scale, w_pa]
    return pl.pallas_call(
        _pool_fold_kernel,
        out_shape=jax.ShapeDtypeStruct((D_POOL, D_MODEL), BF16),
        grid=(1,),
        in_specs=[_const_spec(a.shape) for a in args],
        out_specs=_const_spec((D_POOL, D_MODEL)),
        compiler_params=_params("arbitrary"),
        name="pool_fold",
    )(*args)


def _run_side_by_side(stage_generators):
    live = list(stage_generators)
    while live:
        live = [g for g in live if next(g, StopIteration) is not StopIteration]


def _ffn_kernel(*refs, final, sub, n_cast, tail, tail_pre):
    refs = list(refs)
    x_ref, g_ref, w1_ref, w3_ref, w2_ref = refs[:5]
    refs = refs[5:]
    gf_ref = refs.pop(0) if final else None
    tail_x_ref = refs.pop(0) if tail else None
    tail_o_ref, wco_ref = (refs.pop(0), refs.pop(0)) if tail_pre else (None, None)
    cast_in, refs = refs[:n_cast], refs[n_cast:]
    out_ref = refs.pop(0)
    tail_out_ref = refs.pop(0) if tail else None
    cast_out = refs
    _cast_kernel(*cast_in, *cast_out)

    def sub_tile(x_ref, o_ref, out_ref, r0):
        rows = slice(r0, r0 + sub)
        x = x_ref[rows, :]
        if o_ref is not None:
            x = x + _dot(o_ref[rows, :], wco_ref[...])
        n = _rmsnorm(x, g_ref[...]).astype(BF16)
        yield
        a = _dot(n, w1_ref[...])
        yield
        b = _dot(n, w3_ref[...])
        yield
        mid = (jax.nn.silu(a) * b).astype(BF16)
        y = x + 0.5 * _dot(mid, w2_ref[...])
        if final:
            y = _rmsnorm(y, gf_ref[...])
        out_ref[rows, :] = y

    def tile(x_ref, o_ref, out_ref):
        starts = list(range(0, x_ref.shape[0], sub))
        for i in range(0, len(starts), FFN_SIDE_BY_SIDE):
            _run_side_by_side([sub_tile(x_ref, o_ref, out_ref, r0) for r0 in starts[i:i + FFN_SIDE_BY_SIDE]])

    tile(x_ref, None, out_ref)

    if tail:
        @pl.when(pl.program_id(0) == pl.num_programs(0) - 1)
        def _():
            tile(tail_x_ref, tail_o_ref, tail_out_ref)


def _ffn(x, g, w1, w3, w2, *, g_final=None, tail_x=None, tail_o=None, wco=None, side_cast=(), name):
    rows = x.shape[0]
    tm = min(FFN_ROWS, rows)
    sub = min(FFN_SUB_ROWS, tm)
    assert rows % tm == 0 and tm % sub == 0
    final, tail, tail_pre = g_final is not None, tail_x is not None, tail_o is not None
    row_spec = pl.BlockSpec((tm, D_MODEL), lambda i: (i, 0))
    args = [x, g, w1, w3, w2]
    specs = [row_spec, _const_spec((1, D_MODEL)), _const_spec((D_MODEL, D_FF)), _const_spec((D_MODEL, D_FF)),
             _const_spec((D_FF, D_MODEL))]
    if final:
        args.append(g_final)
        specs.append(_const_spec((1, D_MODEL)))
    out_shapes, out_specs = [jax.ShapeDtypeStruct((rows, D_MODEL), F32)], [row_spec]
    if tail:
        assert tail_x.shape[0] % sub == 0
        args.append(tail_x)
        specs.append(_const_spec(tail_x.shape))
        out_shapes.append(jax.ShapeDtypeStruct(tail_x.shape, F32))
        out_specs.append(_const_spec(tail_x.shape))
    if tail_pre:
        args += [tail_o, wco]
        specs += [_const_spec(tail_o.shape), _const_spec((D_MODEL, D_MODEL))]
    cast_specs = _side_cast_specs(side_cast, rows // tm, lambda i: i)
    outs = pl.pallas_call(
        functools.partial(_ffn_kernel, final=final, sub=sub, n_cast=len(side_cast), tail=tail, tail_pre=tail_pre),
        out_shape=tuple(out_shapes) + tuple(jax.ShapeDtypeStruct(a.shape, BF16) for a in side_cast),
        grid=(rows // tm,),
        in_specs=specs + cast_specs,
        out_specs=tuple(out_specs) + tuple(cast_specs),
        compiler_params=_params("arbitrary"),
        name=name,
    )(*args, *side_cast)
    n_main = len(out_shapes)
    return outs[0], (outs[1] if tail else None), tuple(outs[n_main:])


def _in_proj(h, g_mix, w_in, b_gate):
    n = _rmsnorm(h, g_mix).astype(BF16)
    z = _dot(n, w_in)
    xa = z[:, :D_POOL]
    uv = jax.nn.gelu(z[:, D_POOL:D_POOL + 2 * D_SGU])
    gate = jax.nn.sigmoid(z[:, D_POOL + 2 * D_SGU:] + b_gate)
    return xa, uv[:, :D_SGU], uv[:, D_SGU:], gate[:, :D_MODEL], gate[:, D_MODEL:]


def _pool_project(pooled, xa, wpool_ref, pool_scale):
    mixed = []
    for g in range(POOL_GROUPS):
        sl = slice(g * POOL_GW, (g + 1) * POOL_GW)
        d = (pooled[g] - xa[:, sl]).astype(BF16)
        mixed.append(_dot(d, wpool_ref[g]))
    return jnp.concatenate(mixed, axis=-1) * pool_scale


def _merge(h, a, us, g_a, g_b, wpa_ref, wpb_ref, wo_ref):
    merged = g_a * _dot(a.astype(BF16), wpa_ref[...]) + g_b * _dot(us.astype(BF16), wpb_ref[...])
    return h + _dot(merged.astype(BF16), wo_ref[...])


def _softmax(sc):
    e = jnp.exp(sc - jnp.max(sc, axis=-1, keepdims=True))
    return e / jnp.sum(e, axis=-1, keepdims=True)


def _window_sums(halo, xa_g, window):
    s = jnp.concatenate([halo, xa_g], axis=0)
    step = 1
    while step < window:
        s = s + pltpu.roll(s, step, axis=0)
        step *= 2
    return s[POOL_HALO:]


def _mixer_prompt_kernel(h_ref, qk_ref, vo_ref, gmix_ref, win_ref, bgate_ref, wpoolpa_ref,
                         gsgu_ref, ws_ref, bst_ref, wpb_ref, wo_ref, gca_ref,
                         *rest, tm, sub, n_cast):
    cast_in, (out_ref, pool_ref), cast_out = rest[:n_cast], rest[n_cast:n_cast + 2], rest[n_cast + 2:-1]
    carry_ref = rest[-1]
    _cast_kernel(*cast_in, *cast_out)
    j = pl.program_id(1)

    @pl.when(j == 0)
    def _():
        carry_ref[...] = jnp.zeros((POOL_HALO, D_POOL), F32)

    blk_r = lax.broadcasted_iota(jnp.int32, (SGU_CHUNK, SGU_CHUNK), 0) // CHUNK
    blk_c = lax.broadcasted_iota(jnp.int32, (SGU_CHUNK, SGU_CHUNK), 1) // CHUNK
    wm = [jnp.where(blk_r >= blk_c, ws_ref[hd], 0.0).astype(BF16) for hd in range(SGU_HEADS)]
    n_chunks = sub // SGU_CHUNK

    n_sub = tm // sub
    halos = [carry_ref[...]] + [None] * n_sub

    def sub_tile(i):
        r0 = i * sub
        h = h_ref[0, r0:r0 + sub, :]
        xa, u, v, g_a, g_b = _in_proj(h, gmix_ref[...], win_ref[...], bgate_ref[...])
        halos[i + 1] = xa[sub - POOL_HALO:, :]
        yield

        pos = j * tm + r0 + lax.broadcasted_iota(jnp.int32, (sub, 1), 0)
        d = []
        for g, w in enumerate(POOL_WINDOWS):
            sl = slice(g * POOL_GW, (g + 1) * POOL_GW)
            cnt = jnp.minimum(w, pos + 1).astype(F32)
            d.append(_window_sums(halos[i][:, sl], xa[:, sl], w) / cnt - xa[:, sl])
        a_pa = _dot(jnp.concatenate(d, axis=-1).astype(BF16), wpoolpa_ref[...])
        yield

        vn = _layernorm(v, gsgu_ref[...]).astype(BF16)
        s_heads = []
        for hd in range(SGU_HEADS):
            cols = slice(hd * SGU_HW, (hd + 1) * SGU_HW)
            vcat = jnp.concatenate([vn[c * SGU_CHUNK:(c + 1) * SGU_CHUNK, cols] for c in range(n_chunks)],
                                   axis=-1)
            sh = _dot(wm[hd], vcat) + bst_ref[:, hd:hd + 1]
            s_heads.append(jnp.concatenate([sh[:, c * SGU_HW:(c + 1) * SGU_HW] for c in range(n_chunks)],
                                           axis=0))
        s = jnp.concatenate(s_heads, axis=-1)
        yield

        merged = g_a * a_pa + g_b * _dot((u * s).astype(BF16), wpb_ref[...])
        h = h + _dot(merged.astype(BF16), wo_ref[...])
        yield
        sc = _dot(_rmsnorm(h, gca_ref[...]).astype(BF16), qk_ref[0])
        yield
        p = jnp.concatenate([_softmax(sc[:, hd * N_MEM:(hd + 1) * N_MEM]) for hd in range(MEM_HEADS)],
                            axis=-1).astype(BF16)
        yield
        out_ref[0, r0:r0 + sub, :] = h + _dot(p, vo_ref[0])

    _run_side_by_side([sub_tile(i) for i in range(n_sub)])

    carry_ref[...] = halos[n_sub]

    @pl.when(j == pl.num_programs(1) - 1)
    def _():
        pool_ref[0] = halos[n_sub]


def _mixer_prompt(h, qk, vo, w, side_cast=()):
    nb, seq, _ = h.shape
    tm, sub = MIX_ROWS, MIX_SUB_ROWS
    assert seq % tm == 0 and tm % sub == 0 and sub % SGU_CHUNK == 0
    row_spec = pl.BlockSpec((1, tm, D_MODEL), lambda b, j: (b, j, 0))
    kt_spec = pl.BlockSpec((1, D_MODEL, MEM_HEADS * N_MEM), lambda b, j: (b, 0, 0))
    v_spec = pl.BlockSpec((1, MEM_HEADS * N_MEM, D_MODEL), lambda b, j: (b, 0, 0))
    consts = [w["g_mix"], w["w_in"], w["b_gate"], w["w_pool_pa"], w["g_sgu"], w["w_s"],
              w["b_s_t"], w["w_pb"], w["w_o"], w["g_ca"]]
    tiles = seq // tm
    cast_specs = _side_cast_specs(side_cast, nb * tiles, lambda b, j: b * tiles + j)
    return pl.pallas_call(
        functools.partial(_mixer_prompt_kernel, tm=tm, sub=sub, n_cast=len(side_cast)),
        out_shape=(jax.ShapeDtypeStruct((nb, seq, D_MODEL), F32),
                   jax.ShapeDtypeStruct((nb, POOL_HALO, D_POOL), F32))
        + tuple(jax.ShapeDtypeStruct(a.shape, BF16) for a in side_cast),
        grid=(nb, tiles),
        in_specs=[row_spec, kt_spec, v_spec] + [_const_spec(c.shape) for c in consts] + cast_specs,
        out_specs=(row_spec, pl.BlockSpec((1, POOL_HALO, D_POOL), lambda b, j: (b, 0, 0))) + tuple(cast_specs),
        scratch_shapes=[pltpu.VMEM((POOL_HALO, D_POOL), F32)],
        compiler_params=_params("arbitrary", "arbitrary"),
        name="mixer_prompt",
    )(h, qk, vo, *consts, *side_cast)


def _mixer_sample_kernel(h_ref, state_ref, gmix_ref, win_ref, bgate_ref, wpool_ref, pscale_ref,
                         gsgu_ref, wsx_ref, bsx_ref, wpa_ref, wpb_ref, wo_ref, gca_ref, wq_ref,
                         out_ref, q_ref, pool_ref, vn_ref, *, nb, t, past_len):
    rows = t * nb
    halo = POOL_STATE * nb
    h = h_ref[...]
    xa, u, v, g_a, g_b = _in_proj(h, gmix_ref[...], win_ref[...], bgate_ref[...])

    cat = jnp.concatenate([state_ref[...], xa], axis=0)
    pooled = []
    for g, w in enumerate(POOL_WINDOWS):
        sl = slice(g * POOL_GW, (g + 1) * POOL_GW)
        acc = xa[:, sl]
        for back in range(1, w):
            acc = acc + cat[halo - back * nb:halo - back * nb + rows, sl]
        pos = past_len + lax.broadcasted_iota(jnp.int32, (rows, 1), 0) // nb
        cnt = jnp.minimum(w, pos + 1).astype(F32)
        pooled.append(acc / cnt)
    a = _pool_project(pooled, xa, wpool_ref, pscale_ref[...])
    pool_ref[...] = cat[rows:rows + halo, :]

    vn = _layernorm(v, gsgu_ref[...])
    vn_ref[...] = vn
    s_rows = []
    for p in range(t):
        acc = jnp.broadcast_to(bsx_ref[p:p + 1, :], (nb, D_SGU))
        for qq in range(t):
            if p // CHUNK >= qq // CHUNK:
                acc = acc + wsx_ref[p, qq:qq + 1, :] * vn[qq * nb:(qq + 1) * nb, :]
        s_rows.append(acc)
    s = jnp.concatenate(s_rows, axis=0)

    h = _merge(h, a, u * s, g_a, g_b, wpa_ref, wpb_ref, wo_ref)
    out_ref[...] = h
    q_ref[...] = _dot(_rmsnorm(h, gca_ref[...]).astype(BF16), wq_ref[...]).astype(BF16)


def _mixer_sample(h, state, w, *, nb, t, past_len):
    rows = nb * t
    consts = [w["g_mix"], w["w_in"], w["b_gate"], w["w_pool"], w["pool_scale"], w["g_sgu"], w["w_s_x"],
              w["b_s_x"], w["w_pa"], w["w_pb"], w["w_o"], w["g_ca"], w["w_q"]]
    args = [h, state] + consts
    return pl.pallas_call(
        functools.partial(_mixer_sample_kernel, nb=nb, t=t, past_len=past_len),
        out_shape=(jax.ShapeDtypeStruct((rows, D_MODEL), F32),
                   jax.ShapeDtypeStruct((rows, D_MODEL), BF16),
                   jax.ShapeDtypeStruct((POOL_STATE * nb, D_POOL), F32),
                   jax.ShapeDtypeStruct((rows, D_SGU), F32)),
        grid=(1,),
        in_specs=[_const_spec(a.shape) for a in args],
        out_specs=(_const_spec((rows, D_MODEL)), _const_spec((rows, D_MODEL)),
                   _const_spec((POOL_STATE * nb, D_POOL)), _const_spec((rows, D_SGU))),
        compiler_params=_params("arbitrary"),
        name="mixer_sample",
    )(*args)


def _attn_sample_kernel(q_ref, k_ref, v_ref, o_ref, *, streams):
    t = q_ref.shape[1]
    n_cols = N_MEM * MEM_HEADS
    row_head = lax.broadcasted_iota(jnp.int32, (MEM_HEADS * t, n_cols), 0) // t
    col_head = lax.broadcasted_iota(jnp.int32, (MEM_HEADS * t, n_cols), 1) % MEM_HEADS
    own_head = row_head == col_head

    def stream(s):
        q = q_ref[s]
        q_rows = jnp.concatenate([q[:, hd * MEM_HEAD_DIM:(hd + 1) * MEM_HEAD_DIM] for hd in range(MEM_HEADS)],
                                 axis=0)
        k = k_ref[0, s].reshape(n_cols, MEM_HEAD_DIM).astype(BF16)
        v = v_ref[0, s].reshape(n_cols, MEM_HEAD_DIM).astype(BF16)
        yield
        sc = lax.dot_general(q_rows, k, (((1,), (1,)), ((), ())), preferred_element_type=F32)
        sc = jnp.where(own_head, sc * (MEM_HEAD_DIM ** -0.5), -jnp.inf)
        yield
        p = _softmax(sc)
        yield
        o = _dot(p.astype(BF16), v)
        o_ref[s] = jnp.concatenate([o[hd * t:(hd + 1) * t, :] for hd in range(MEM_HEADS)],
                                   axis=-1).astype(BF16)

    _run_side_by_side([stream(s) for s in range(streams)])


def _attn_sample(q, k, v):
    nb, t, _ = q.shape
    streams = ATTN_SAMPLE_STREAMS
    assert nb % streams == 0
    q_spec = pl.BlockSpec((streams, t, D_MODEL), lambda b: (b, 0, 0))
    kv_spec = pl.BlockSpec((1, streams, N_MEM, MEM_HEADS, MEM_HEAD_DIM), lambda b: (0, b, 0, 0, 0))
    return pl.pallas_call(
        functools.partial(_attn_sample_kernel, streams=streams),
        out_shape=jax.ShapeDtypeStruct((nb, t, D_MODEL), BF16),
        grid=(nb // streams,),
        in_specs=[q_spec, kv_spec, kv_spec],
        out_specs=q_spec,
        compiler_params=_params("arbitrary"),
        name="attn_sample",
    )(q, k, v)


def kernel(x_prompt, x_sample, state_pool, cache_mem_k, cache_mem_v, mem_prompt, g_ff1, w1a, w3a, w2a,
           g_mix, w_in, b_gate, w_pool, pool_scale, g_sgu, w_s, b_s, w_pa, w_pb, w_o, g_mem, w_mk, w_mv,
           g_ca, w_q, w_co, g_ff2, w1b, w3b, w2b, g_final):
    nb, seq, _ = x_prompt.shape
    nbs, t, _ = x_sample.shape
    depth = g_ff1.shape[0]
    assert depth == 1
    l = 0
    row = lambda a: a.reshape(1, -1)
    mat = lambda a: a[l].reshape(-1, a.shape[-1])
    w1a_b, w3a_b, w2a_b = _cast_bf16([mat(a) for a in (w1a, w3a, w2a)])

    xs = jnp.transpose(x_sample, (1, 0, 2)).reshape(t * nbs, D_MODEL)
    st = jnp.transpose(state_pool[l], (1, 0, 2)).reshape(POOL_STATE * nbs, D_POOL)
    to_stream_major = lambda a, n: jnp.transpose(a.reshape(n, nbs, a.shape[-1]), (1, 0, 2))

    hp, hs, (w_in_b, w_pa_b, w_pb_b, w_o_b, w_q_b, w_co_b, w_pool_b, w_mk_b, w_mv_b) = _ffn(
        x_prompt.reshape(nb * seq, D_MODEL), row(g_ff1[l]), w1a_b, w3a_b, w2a_b, tail_x=xs,
        side_cast=[mat(a) for a in (w_in, w_pa, w_pb, w_o, w_q, w_co, w_pool, w_mk, w_mv)], name="ffn1")
    mk, mv, qk, vo = _mem_kv(mem_prompt, row(g_mem[l]), w_mk_b, w_mv_b, w_q_b, w_co_b)
    w = dict(
        g_mix=row(g_mix[l]), w_in=w_in_b, b_gate=row(b_gate[l]),
        w_pool=w_pool_b.reshape(POOL_GROUPS, POOL_GW, POOL_GW),
        w_pool_pa=_pool_fold(w_pool[l], row(pool_scale[l]), w_pa[l]),
        pool_scale=row(pool_scale[l]), g_sgu=row(g_sgu[l]), w_s=w_s[l], b_s_t=b_s[l].T,
        w_pa=w_pa_b, w_pb=w_pb_b, w_o=w_o_b, g_ca=row(g_ca[l]), w_q=w_q_b,
        w_s_x=jnp.repeat(jnp.transpose(w_s[l][:, :t, :t], (1, 2, 0)), SGU_HW, axis=-1),
        b_s_x=jnp.repeat(b_s[l][:, :t].T, SGU_HW, axis=-1),
    )

    hs, qs, pool_s, vn_s = _mixer_sample(hs, st, w, nb=nbs, t=t, past_len=PAST_LEN)
    hs = to_stream_major(hs, t).reshape(nbs * t, D_MODEL)
    os_ = _attn_sample(to_stream_major(qs, t), cache_mem_k, cache_mem_v).reshape(nbs * t, D_MODEL)
    hp, pool_p, w1b_b, w3b_b, w2b_b = _mixer_prompt(hp.reshape(nb, seq, D_MODEL), qk, vo, w,
                                                    side_cast=[mat(a) for a in (w1b, w3b, w2b)])

    y_prompt, y_sample, _ = _ffn(hp.reshape(nb * seq, D_MODEL), row(g_ff2[l]), w1b_b, w3b_b, w2b_b,
                                 g_final=row(g_final), tail_x=hs, tail_o=os_, wco=w_co_b, name="ffn2")

    pool_prompt = pool_p[:, POOL_HALO - POOL_STATE:, :][None]
    pool_sample = to_stream_major(pool_s, POOL_STATE)[None]
    sgu_v_sample = to_stream_major(vn_s, t)[None]
    return (y_prompt.reshape(nb, seq, D_MODEL), y_sample.reshape(nbs, t, D_MODEL), pool_prompt, pool_sample,
            sgu_v_sample, mk, mv)
```

```python
import functools

import jax
import jax.numpy as jnp
from jax import lax
from jax.experimental import pallas as pl
from jax.experimental.pallas import tpu as pltpu

D_MODEL = 1024
PAST_LEN = 1024
CHUNK = 64
N_MEM = 256
MEM_HEADS = 4
MEM_HEAD_DIM = D_MODEL // MEM_HEADS
D_POOL = D_MODEL // 2
POOL_WINDOWS = (2, 4, 8, 16)
POOL_GROUPS = len(POOL_WINDOWS)
POOL_GW = D_POOL // POOL_GROUPS
POOL_STATE = max(POOL_WINDOWS) - 1
D_SGU = D_MODEL // 2
SGU_HEADS = 4
SGU_HW = D_SGU // SGU_HEADS
SGU_CHUNK = 128
D_FF = 2816
D_IN = D_POOL + 2 * D_SGU + 2 * D_MODEL
EPS = 1e-6

V7X_VMEM_LIMIT_BYTES = 56 * 1024 * 1024
F32_SUBLANES = 8
BF16_SUBLANES = 16
LANES = 128
CAST_STEPS = 16
POOL_HALO = 2 * F32_SUBLANES
assert POOL_HALO >= POOL_STATE + 1

FFN_ROWS = 1024
FFN_SUB_ROWS = 512
FFN_SIDE_BY_SIDE = 1
MIX_ROWS = 512
MIX_SUB_ROWS = 256
ATTN_SAMPLE_STREAMS = 4

BF16 = jnp.bfloat16
F32 = jnp.float32


def _dot(a, b):
    return jnp.dot(a, b, preferred_element_type=F32)


def _rmsnorm(x, g):
    y = x * lax.rsqrt(jnp.mean(x * x, axis=-1, keepdims=True) + EPS)
    return y * g


def _layernorm(x, g):
    mu = jnp.mean(x, axis=-1, keepdims=True)
    xc = x - mu
    y = xc * lax.rsqrt(jnp.mean(xc * xc, axis=-1, keepdims=True) + EPS)
    return y * g


def _const_spec(shape):
    nd = len(shape)
    return pl.BlockSpec(shape, lambda *_: (0,) * nd, pipeline_mode=pl.Buffered(1))


def _params(*sem):
    return pltpu.CompilerParams(dimension_semantics=sem, vmem_limit_bytes=V7X_VMEM_LIMIT_BYTES)


def _cast_kernel(*refs):
    n = len(refs) // 2
    for src, dst in zip(refs[:n], refs[n:]):
        dst[...] = src[...].astype(BF16)


def _side_cast_specs(arrays, steps, flat_step):
    specs = []
    for a in arrays:
        rows = a.shape[0]
        n = next(n for n in range(steps, 0, -1) if rows % (n * BF16_SUBLANES) == 0)
        specs.append(pl.BlockSpec((rows // n, a.shape[1]),
                                  lambda *g, n=n: (jnp.minimum(flat_step(*g), n - 1), 0)))
    return specs


def _cast_bf16(arrays):
    for a in arrays:
        assert a.ndim == 2 and a.shape[0] % (CAST_STEPS * BF16_SUBLANES) == 0 and a.shape[1] % LANES == 0
    specs = [pl.BlockSpec((a.shape[0] // CAST_STEPS, a.shape[1]), lambda i: (i, 0)) for a in arrays]
    return pl.pallas_call(
        _cast_kernel,
        out_shape=tuple(jax.ShapeDtypeStruct(a.shape, BF16) for a in arrays),
        grid=(CAST_STEPS,),
        in_specs=specs,
        out_specs=tuple(specs),
        compiler_params=_params("arbitrary"),
        name="cast_weights",
    )(*arrays)


def _mem_kv_kernel(mem_ref, g_ref, wk_ref, wv_ref, wq_ref, wco_ref, k_ref, v_ref, qk_ref, vo_ref):
    mn = _rmsnorm(mem_ref[0], g_ref[...]).astype(BF16)
    k = _dot(mn, wk_ref[...])
    v = _dot(mn, wv_ref[...])
    kt = k.T.astype(BF16)
    vb = v.astype(BF16)
    for hd in range(MEM_HEADS):
        sl = slice(hd * MEM_HEAD_DIM, (hd + 1) * MEM_HEAD_DIM)
        mem = slice(hd * N_MEM, (hd + 1) * N_MEM)
        k_ref[0, 0, :, hd, :] = k[:, sl]
        v_ref[0, 0, :, hd, :] = v[:, sl]
        qk_ref[0, :, mem] = (_dot(wq_ref[:, sl], kt[sl, :]) * (MEM_HEAD_DIM ** -0.5)).astype(BF16)
        vo_ref[0, mem, :] = _dot(vb[:, sl], wco_ref[sl, :]).astype(BF16)


def _mem_kv(mem, g, wk, wv, wq, wco):
    nb = mem.shape[0]
    blk = pl.BlockSpec((1, N_MEM, D_MODEL), lambda b: (b, 0, 0))
    blk_qk = pl.BlockSpec((1, D_MODEL, MEM_HEADS * N_MEM), lambda b: (b, 0, 0))
    blk_vo = pl.BlockSpec((1, MEM_HEADS * N_MEM, D_MODEL), lambda b: (b, 0, 0))
    blk_heads = pl.BlockSpec((1, 1, N_MEM, MEM_HEADS, MEM_HEAD_DIM), lambda b: (0, b, 0, 0, 0))
    square = _const_spec((D_MODEL, D_MODEL))
    return pl.pallas_call(
        _mem_kv_kernel,
        out_shape=(jax.ShapeDtypeStruct((1, nb, N_MEM, MEM_HEADS, MEM_HEAD_DIM), F32),) * 2
        + (jax.ShapeDtypeStruct((nb, D_MODEL, MEM_HEADS * N_MEM), BF16),
           jax.ShapeDtypeStruct((nb, MEM_HEADS * N_MEM, D_MODEL), BF16)),
        grid=(nb,),
        in_specs=[blk, _const_spec((1, D_MODEL)), square, square, square, square],
        out_specs=(blk_heads, blk_heads, blk_qk, blk_vo),
        compiler_params=_params("arbitrary"),
        name="mem_kv",
    )(mem, g, wk, wv, wq, wco)


def _pool_fold_kernel(wpool_ref, pscale_ref, wpa_ref, out_ref):
    for g in range(POOL_GROUPS):
        sl = slice(g * POOL_GW, (g + 1) * POOL_GW)
        scaled = (wpool_ref[g] * pscale_ref[:, sl]).astype(BF16)
        out_ref[sl, :] = _dot(scaled, wpa_ref[sl, :].astype(BF16)).astype(BF16)


def _pool_fold(w_pool, pool_scale, w_pa):
    args = [w_pool, pool_scale, w_pa]
    return pl.pallas_call(
        _pool_fold_kernel,
        out_shape=jax.ShapeDtypeStruct((D_POOL, D_MODEL), BF16),
        grid=(1,),
        in_specs=[_const_spec(a.shape) for a in args],
        out_specs=_const_spec((D_POOL, D_MODEL)),
        compiler_params=_params("arbitrary"),
        name="pool_fold",
    )(*args)


def _run_side_by_side(stage_generators):
    live = list(stage_generators)
    while live:
        live = [g for g in live if next(g, StopIteration) is not StopIteration]


def _ffn_kernel(*refs, final, sub, n_cast, tail, tail_pre):
    refs = list(refs)
    x_ref, g_ref, w1_ref, w3_ref, w2_ref = refs[:5]
    refs = refs[5:]
    gf_ref = refs.pop(0) if final else None
    tail_x_ref = refs.pop(0) if tail else None
    tail_o_ref, wco_ref = (refs.pop(0), refs.pop(0)) if tail_pre else (None, None)
    cast_in, refs = refs[:n_cast], refs[n_cast:]
    out_ref = refs.pop(0)
    tail_out_ref = refs.pop(0) if tail else None
    cast_out = refs
    _cast_kernel(*cast_in, *cast_out)

    def sub_tile(x_ref, o_ref, out_ref, r0):
        rows = slice(r0, r0 + sub)
        x = x_ref[rows, :]
        if o_ref is not None:
            x = x + _dot(o_ref[rows, :], wco_ref[...])
        n = _rmsnorm(x, g_ref[...]).astype(BF16)
        yield
        a = _dot(n, w1_ref[...])
        yield
        b = _dot(n, w3_ref[...])
        yield
        mid = (jax.nn.silu(a) * b).astype(BF16)
        y = x + 0.5 * _dot(mid, w2_ref[...])
        if final:
            y = _rmsnorm(y, gf_ref[...])
        out_ref[rows, :] = y

    def tile(x_ref, o_ref, out_ref):
        starts = list(range(0, x_ref.shape[0], sub))
        for i in range(0, len(starts), FFN_SIDE_BY_SIDE):
            _run_side_by_side([sub_tile(x_ref, o_ref, out_ref, r0) for r0 in starts[i:i + FFN_SIDE_BY_SIDE]])

    tile(x_ref, None, out_ref)

    if tail:
        @pl.when(pl.program_id(0) == pl.num_programs(0) - 1)
        def _():
            tile(tail_x_ref, tail_o_ref, tail_out_ref)


def _ffn(x, g, w1, w3, w2, *, g_final=None, tail_x=None, tail_o=None, wco=None, side_cast=(), name):
    rows = x.shape[0]
    tm = min(FFN_ROWS, rows)
    sub = min(FFN_SUB_ROWS, tm)
    assert rows % tm == 0 and tm % sub == 0
    final, tail, tail_pre = g_final is not None, tail_x is not None, tail_o is not None
    row_spec = pl.BlockSpec((tm, D_MODEL), lambda i: (i, 0))
    args = [x, g, w1, w3, w2]
    specs = [row_spec, _const_spec((1, D_MODEL)), _const_spec((D_MODEL, D_FF)), _const_spec((D_MODEL, D_FF)),
             _const_spec((D_FF, D_MODEL))]
    if final:
        args.append(g_final)
        specs.append(_const_spec((1, D_MODEL)))
    out_shapes, out_specs = [jax.ShapeDtypeStruct((rows, D_MODEL), F32)], [row_spec]
    if tail:
        assert tail_x.shape[0] % sub == 0
        args.append(tail_x)
        specs.append(_const_spec(tail_x.shape))
        out_shapes.append(jax.ShapeDtypeStruct(tail_x.shape, F32))
        out_specs.append(_const_spec(tail_x.shape))
    if tail_pre:
        args += [tail_o, wco]
        specs += [_const_spec(tail_o.shape), _const_spec((D_MODEL, D_MODEL))]
    cast_specs = _side_cast_specs(side_cast, rows // tm, lambda i: i)
    outs = pl.pallas_call(
        functools.partial(_ffn_kernel, final=final, sub=sub, n_cast=len(side_cast), tail=tail, tail_pre=tail_pre),
        out_shape=tuple(out_shapes) + tuple(jax.ShapeDtypeStruct(a.shape, BF16) for a in side_cast),
        grid=(rows // tm,),
        in_specs=specs + cast_specs,
        out_specs=tuple(out_specs) + tuple(cast_specs),
        compiler_params=_params("arbitrary"),
        name=name,
    )(*args, *side_cast)
    n_main = len(out_shapes)
    return outs[0], (outs[1] if tail else None), tuple(outs[n_main:])


def _in_proj(h, g_mix, w_in, b_gate):
    n = _rmsnorm(h, g_mix).astype(BF16)
    z = _dot(n, w_in)
    xa = z[:, :D_POOL]
    uv = jax.nn.gelu(z[:, D_POOL:D_POOL + 2 * D_SGU])
    gate = jax.nn.sigmoid(z[:, D_POOL + 2 * D_SGU:] + b_gate)
    return xa, uv[:, :D_SGU], uv[:, D_SGU:], gate[:, :D_MODEL], gate[:, D_MODEL:]


def _pool_project(pooled, xa, wpool_ref, pool_scale):
    mixed = []
    for g in range(POOL_GROUPS):
        sl = slice(g * POOL_GW, (g + 1) * POOL_GW)
        d = (pooled[g] - xa[:, sl]).astype(BF16)
        mixed.append(_dot(d, wpool_ref[g]))
    return jnp.concatenate(mixed, axis=-1) * pool_scale


def _merge(h, a, us, g_a, g_b, wpa_ref, wpb_ref, wo_ref):
    merged = g_a * _dot(a.astype(BF16), wpa_ref[...]) + g_b * _dot(us.astype(BF16), wpb_ref[...])
    return h + _dot(merged.astype(BF16), wo_ref[...])


def _softmax(sc):
    e = jnp.exp(sc - jnp.max(sc, axis=-1, keepdims=True))
    return e / jnp.sum(e, axis=-1, keepdims=True)


def _window_sums(halo, xa_g, window):
    s = jnp.concatenate([halo, xa_g], axis=0)
    step = 1
    while step < window:
        s = s + pltpu.roll(s, step, axis=0)
        step *= 2
    return s[POOL_HALO:]


def _mixer_prompt_kernel(h_ref, qk_ref, vo_ref, gmix_ref, win_ref, bgate_ref, wpoolpa_ref,
                         gsgu_ref, ws_ref, bst_ref, wpb_ref, wo_ref, gca_ref,
                         *rest, tm, sub, n_cast):
    cast_in, (out_ref, pool_ref), cast_out = rest[:n_cast], rest[n_cast:n_cast + 2], rest[n_cast + 2:-1]
    carry_ref = rest[-1]
    _cast_kernel(*cast_in, *cast_out)
    j = pl.program_id(1)

    @pl.when(j == 0)
    def _():
        carry_ref[...] = jnp.zeros((POOL_HALO, D_POOL), F32)

    blk_r = lax.broadcasted_iota(jnp.int32, (SGU_CHUNK, SGU_CHUNK), 0) // CHUNK
    blk_c = lax.broadcasted_iota(jnp.int32, (SGU_CHUNK, SGU_CHUNK), 1) // CHUNK
    wm = [jnp.where(blk_r >= blk_c, ws_ref[hd], 0.0).astype(BF16) for hd in range(SGU_HEADS)]
    n_chunks = sub // SGU_CHUNK

    n_sub = tm // sub
    halos = [carry_ref[...]] + [None] * n_sub

    def sub_tile(i):
        r0 = i * sub
        h = h_ref[0, r0:r0 + sub, :]
        xa, u, v, g_a, g_b = _in_proj(h, gmix_ref[...], win_ref[...], bgate_ref[...])
        halos[i + 1] = xa[sub - POOL_HALO:, :]
        yield

        pos = j * tm + r0 + lax.broadcasted_iota(jnp.int32, (sub, 1), 0)
        d = []
        for g, w in enumerate(POOL_WINDOWS):
            sl = slice(g * POOL_GW, (g + 1) * POOL_GW)
            cnt = jnp.minimum(w, pos + 1).astype(F32)
            d.append(_window_sums(halos[i][:, sl], xa[:, sl], w) / cnt - xa[:, sl])
        a_pa = _dot(jnp.concatenate(d, axis=-1).astype(BF16), wpoolpa_ref[...])
        yield

        vn = _layernorm(v, gsgu_ref[...]).astype(BF16)
        s_heads = []
        for hd in range(SGU_HEADS):
            cols = slice(hd * SGU_HW, (hd + 1) * SGU_HW)
            vcat = jnp.concatenate([vn[c * SGU_CHUNK:(c + 1) * SGU_CHUNK, cols] for c in range(n_chunks)],
                                   axis=-1)
            sh = _dot(wm[hd], vcat) + bst_ref[:, hd:hd + 1]
            s_heads.append(jnp.concatenate([sh[:, c * SGU_HW:(c + 1) * SGU_HW] for c in range(n_chunks)],
                                           axis=0))
        s = jnp.concatenate(s_heads, axis=-1)
        yield

        merged = g_a * a_pa + g_b * _dot((u * s).astype(BF16), wpb_ref[...])
        h = h + _dot(merged.astype(BF16), wo_ref[...])
        yield
        sc = _dot(_rmsnorm(h, gca_ref[...]).astype(BF16), qk_ref[0])
        yield
        p = jnp.concatenate([_softmax(sc[:, hd * N_MEM:(hd + 1) * N_MEM]) for hd in range(MEM_HEADS)],
                            axis=-1).astype(BF16)
        yield
        out_ref[0, r0:r0 + sub, :] = h + _dot(p, vo_ref[0])

    _run_side_by_side([sub_tile(i) for i in range(n_sub)])

    carry_ref[...] = halos[n_sub]

    @pl.when(j == pl.num_programs(1) - 1)
    def _():
        pool_ref[0] = halos[n_sub]


def _mixer_prompt(h, qk, vo, w, side_cast=()):
    nb, seq, _ = h.shape
    tm, sub = MIX_ROWS, MIX_SUB_ROWS
    assert seq % tm == 0 and tm % sub == 0 and sub % SGU_CHUNK == 0
    row_spec = pl.BlockSpec((1, tm, D_MODEL), lambda b, j: (b, j, 0))
    kt_spec = pl.BlockSpec((1, D_MODEL, MEM_HEADS * N_MEM), lambda b, j: (b, 0, 0))
    v_spec = pl.BlockSpec((1, MEM_HEADS * N_MEM, D_MODEL), lambda b, j: (b, 0, 0))
    consts = [w["g_mix"], w["w_in"], w["b_gate"], w["w_pool_pa"], w["g_sgu"], w["w_s"],
              w["b_s_t"], w["w_pb"], w["w_o"], w["g_ca"]]
    tiles = seq // tm
    cast_specs = _side_cast_specs(side_cast, nb * tiles, lambda b, j: b * tiles + j)
    return pl.pallas_call(
        functools.partial(_mixer_prompt_kernel, tm=tm, sub=sub, n_cast=len(side_cast)),
        out_shape=(jax.ShapeDtypeStruct((nb, seq, D_MODEL), F32),
                   jax.ShapeDtypeStruct((nb, POOL_HALO, D_POOL), F32))
        + tuple(jax.ShapeDtypeStruct(a.shape, BF16) for a in side_cast),
        grid=(nb, tiles),
        in_specs=[row_spec, kt_spec, v_spec] + [_const_spec(c.shape) for c in consts] + cast_specs,
        out_specs=(row_spec, pl.BlockSpec((1, POOL_HALO, D_POOL), lambda b, j: (b, 0, 0))) + tuple(cast_specs),
        scratch_shapes=[pltpu.VMEM((POOL_HALO, D_POOL), F32)],
        compiler_params=_params("arbitrary", "arbitrary"),
        name="mixer_prompt",
    )(h, qk, vo, *consts, *side_cast)


def _mixer_sample_kernel(h_ref, state_ref, gmix_ref, win_ref, bgate_ref, wpool_ref, pscale_ref,
                         gsgu_ref, wsx_ref, bsx_ref, wpa_ref, wpb_ref, wo_ref, gca_ref, wq_ref,
                         out_ref, q_ref, pool_ref, vn_ref, *, nb, t, past_len):
    rows = t * nb
    halo = POOL_STATE * nb
    h = h_ref[...]
    xa, u, v, g_a, g_b = _in_proj(h, gmix_ref[...], win_ref[...], bgate_ref[...])

    cat = jnp.concatenate([state_ref[...], xa], axis=0)
    pooled = []
    for g, w in enumerate(POOL_WINDOWS):
        sl = slice(g * POOL_GW, (g + 1) * POOL_GW)
        acc = xa[:, sl]
        for back in range(1, w):
            acc = acc + cat[halo - back * nb:halo - back * nb + rows, sl]
        pos = past_len + lax.broadcasted_iota(jnp.int32, (rows, 1), 0) // nb
        cnt = jnp.minimum(w, pos + 1).astype(F32)
        pooled.append(acc / cnt)
    a = _pool_project(pooled, xa, wpool_ref, pscale_ref[...])
    pool_ref[...] = cat[rows:rows + halo, :]

    vn = _layernorm(v, gsgu_ref[...])
    vn_ref[...] = vn
    s_rows = []
    for p in range(t):
        acc = jnp.broadcast_to(bsx_ref[p:p + 1, :], (nb, D_SGU))
        for qq in range(t):
            if p // CHUNK >= qq // CHUNK:
                acc = acc + wsx_ref[p, qq:qq + 1, :] * vn[qq * nb:(qq + 1) * nb, :]
        s_rows.append(acc)
    s = jnp.concatenate(s_rows, axis=0)

    h = _merge(h, a, u * s, g_a, g_b, wpa_ref, wpb_ref, wo_ref)
    out_ref[...] = h
    q_ref[...] = _dot(_rmsnorm(h, gca_ref[...]).astype(BF16), wq_ref[...]).astype(BF16)


def _mixer_sample(h, state, w, *, nb, t, past_len):
    rows = nb * t
    consts = [w["g_mix"], w["w_in"], w["b_gate"], w["w_pool"], w["pool_scale"], w["g_sgu"], w["w_s_x"],
              w["b_s_x"], w["w_pa"], w["w_pb"], w["w_o"], w["g_ca"], w["w_q"]]
    args = [h, state] + consts
    return pl.pallas_call(
        functools.partial(_mixer_sample_kernel, nb=nb, t=t, past_len=past_len),
        out_shape=(jax.ShapeDtypeStruct((rows, D_MODEL), F32),
                   jax.ShapeDtypeStruct((rows, D_MODEL), BF16),
                   jax.ShapeDtypeStruct((POOL_STATE * nb, D_POOL), F32),
                   jax.ShapeDtypeStruct((rows, D_SGU), F32)),
        grid=(1,),
        in_specs=[_const_spec(a.shape) for a in args],
        out_specs=(_const_spec((rows, D_MODEL)), _const_spec((rows, D_MODEL)),
                   _const_spec((POOL_STATE * nb, D_POOL)), _const_spec((rows, D_SGU))),
        compiler_params=_params("arbitrary"),
        name="mixer_sample",
    )(*args)


def _attn_sample_kernel(q_ref, k_ref, v_ref, o_ref, *, streams):
    t = q_ref.shape[1]
    n_cols = N_MEM * MEM_HEADS
    row_head = lax.broadcasted_iota(jnp.int32, (MEM_HEADS * t, n_cols), 0) // t
    col_head = lax.broadcasted_iota(jnp.int32, (MEM_HEADS * t, n_cols), 1) % MEM_HEADS
    own_head = row_head == col_head

    def stream(s):
        q = q_ref[s]
        q_rows = jnp.concatenate([q[:, hd * MEM_HEAD_DIM:(hd + 1) * MEM_HEAD_DIM] for hd in range(MEM_HEADS)],
                                 axis=0)
        k = k_ref[0, s].reshape(n_cols, MEM_HEAD_DIM).astype(BF16)
        v = v_ref[0, s].reshape(n_cols, MEM_HEAD_DIM).astype(BF16)
        yield
        sc = lax.dot_general(q_rows, k, (((1,), (1,)), ((), ())), preferred_element_type=F32)
        sc = jnp.where(own_head, sc * (MEM_HEAD_DIM ** -0.5), -jnp.inf)
        yield
        p = _softmax(sc)
        yield
        o = _dot(p.astype(BF16), v)
        o_ref[s] = jnp.concatenate([o[hd * t:(hd + 1) * t, :] for hd in range(MEM_HEADS)],
                                   axis=-1).astype(BF16)

    _run_side_by_side([stream(s) for s in range(streams)])


def _attn_sample(q, k, v):
    nb, t, _ = q.shape
    streams = ATTN_SAMPLE_STREAMS
    assert nb % streams == 0
    q_spec = pl.BlockSpec((streams, t, D_MODEL), lambda b: (b, 0, 0))
    kv_spec = pl.BlockSpec((1, streams, N_MEM, MEM_HEADS, MEM_HEAD_DIM), lambda b: (0, b, 0, 0, 0))
    return pl.pallas_call(
        functools.partial(_attn_sample_kernel, streams=streams),
        out_shape=jax.ShapeDtypeStruct((nb, t, D_MODEL), BF16),
        grid=(nb // streams,),
        in_specs=[q_spec, kv_spec, kv_spec],
        out_specs=q_spec,
        compiler_params=_params("arbitrary"),
        name="attn_sample",
    )(q, k, v)


def kernel(x_prompt, x_sample, state_pool, cache_mem_k, cache_mem_v, mem_prompt, g_ff1, w1a, w3a, w2a,
           g_mix, w_in, b_gate, w_pool, pool_scale, g_sgu, w_s, b_s, w_pa, w_pb, w_o, g_mem, w_mk, w_mv,
           g_ca, w_q, w_co, g_ff2, w1b, w3b, w2b, g_final):
    nb, seq, _ = x_prompt.shape
    nbs, t, _ = x_sample.shape
    depth = g_ff1.shape[0]
    assert depth == 1
    l = 0
    row = lambda a: a.reshape(1, -1)
    mat = lambda a: a[l].reshape(-1, a.shape[-1])
    w1a_b, w3a_b, w2a_b = _cast_bf16([mat(a) for a in (w1a, w3a, w2a)])

    xs = jnp.transpose(x_sample, (1, 0, 2)).reshape(t * nbs, D_MODEL)
    st = jnp.transpose(state_pool[l], (1, 0, 2)).reshape(POOL_STATE * nbs, D_POOL)
    to_stream_major = lambda a, n: jnp.transpose(a.reshape(n, nbs, a.shape[-1]), (1, 0, 2))

    hp, hs, (w_in_b, w_pa_b, w_pb_b, w_o_b, w_q_b, w_co_b, w_pool_b, w_mk_b, w_mv_b) = _ffn(
        x_prompt.reshape(nb * seq, D_MODEL), row(g_ff1[l]), w1a_b, w3a_b, w2a_b, tail_x=xs,
        side_cast=[mat(a) for a in (w_in, w_pa, w_pb, w_o, w_q, w_co, w_pool, w_mk, w_mv)], name="ffn1")
    mk, mv, qk, vo = _mem_kv(mem_prompt, row(g_mem[l]), w_mk_b, w_mv_b, w_q_b, w_co_b)
    w = dict(
        g_mix=row(g_mix[l]), w_in=w_in_b, b_gate=row(b_gate[l]),
        w_pool=w_pool_b.reshape(POOL_GROUPS, POOL_GW, POOL_GW),
        w_pool_pa=_pool_fold(w_pool[l], row(pool_scale[l]), w_pa[l]),
        pool_scale=row(pool_scale[l]), g_sgu=row(g_sgu[l]), w_s=w_s[l], b_s_t=b_s[l].T,
        w_pa=w_pa_b, w_pb=w_pb_b, w_o=w_o_b, g_ca=row(g_ca[l]), w_q=w_q_b,
        w_s_x=jnp.repeat(jnp.transpose(w_s[l][:, :t, :t], (1, 2, 0)), SGU_HW, axis=-1),
        b_s_x=jnp.repeat(b_s[l][:, :t].T, SGU_HW, axis=-1),
    )

    hs, qs, pool_s, vn_s = _mixer_sample(hs, st, w, nb=nbs, t=t, past_len=PAST_LEN)
    hs = to_stream_major(hs, t).reshape(nbs * t, D_MODEL)
    os_ = _attn_sample(to_stream_major(qs, t), cache_mem_k, cache_mem_v).reshape(nbs * t, D_MODEL)
    hp, pool_p, w1b_b, w3b_b, w2b_b = _mixer_prompt(hp.reshape(nb, seq, D_MODEL), qk, vo, w,
                                                    side_cast=[mat(a) for a in (w1b, w3b, w2b)])

    y_prompt, y_sample, _ = _ffn(hp.reshape(nb * seq, D_MODEL), row(g_ff2[l]), w1b_b, w3b_b, w2b_b,
                                 g_final=row(g_final), tail_x=hs, tail_o=os_, wco=w_co_b, name="ffn2")

    pool_prompt = pool_p[:, POOL_HALO - POOL_STATE:, :][None]
    pool_sample = to_stream_major(pool_s, POOL_STATE)[None]
    sgu_v_sample = to_stream_major(vn_s, t)[None]
    return (y_prompt.reshape(nb, seq, D_MODEL), y_sample.reshape(nbs, t, D_MODEL), pool_prompt, pool_sample,
            sgu_v_sample, mk, mv)
```

```python
import functools

import jax
import jax.numpy as jnp
from jax import lax
from jax.experimental import pallas as pl
from jax.experimental.pallas import tpu as pltpu

D_MODEL = 1024
PAST_LEN = 1024
CHUNK = 64
N_MEM = 256
MEM_HEADS = 4
MEM_HEAD_DIM = D_MODEL // MEM_HEADS
D_POOL = D_MODEL // 2
POOL_WINDOWS = (2, 4, 8, 16)
POOL_GROUPS = len(POOL_WINDOWS)
POOL_GW = D_POOL // POOL_GROUPS
POOL_STATE = max(POOL_WINDOWS) - 1
D_SGU = D_MODEL // 2
SGU_HEADS = 4
SGU_HW = D_SGU // SGU_HEADS
SGU_CHUNK = 128
D_FF = 2816
D_IN = D_POOL + 2 * D_SGU + 2 * D_MODEL
EPS = 1e-6

V7X_VMEM_LIMIT_BYTES = 56 * 1024 * 1024
F32_SUBLANES = 8
BF16_SUBLANES = 16
LANES = 128
CAST_STEPS = 8
POOL_HALO = 2 * F32_SUBLANES
assert POOL_HALO >= POOL_STATE + 1

FFN_ROWS = 1024
FFN_SUB_ROWS = 256
FFN_SIDE_BY_SIDE = 2
MIX_ROWS = 512
MIX_SUB_ROWS = 256
ATTN_SAMPLE_STREAMS = 8

BF16 = jnp.bfloat16
F32 = jnp.float32


def _dot(a, b):
    return jnp.dot(a, b, preferred_element_type=F32)


def _rmsnorm(x, g):
    y = x * lax.rsqrt(jnp.mean(x * x, axis=-1, keepdims=True) + EPS)
    return y * g


def _layernorm(x, g):
    mu = jnp.mean(x, axis=-1, keepdims=True)
    xc = x - mu
    y = xc * lax.rsqrt(jnp.mean(xc * xc, axis=-1, keepdims=True) + EPS)
    return y * g


def _const_spec(shape):
    nd = len(shape)
    return pl.BlockSpec(shape, lambda *_: (0,) * nd, pipeline_mode=pl.Buffered(1))


def _params(*sem):
    return pltpu.CompilerParams(dimension_semantics=sem, vmem_limit_bytes=V7X_VMEM_LIMIT_BYTES)


def _cast_kernel(*refs):
    n = len(refs) // 2
    for src, dst in zip(refs[:n], refs[n:]):
        dst[...] = src[...].astype(BF16)


def _side_cast_specs(arrays, steps, flat_step):
    specs = []
    for a in arrays:
        rows = a.shape[0]
        n = next(n for n in range(steps, 0, -1) if rows % (n * BF16_SUBLANES) == 0)
        specs.append(pl.BlockSpec((rows // n, a.shape[1]),
                                  lambda *g, n=n: (jnp.minimum(flat_step(*g), n - 1), 0)))
    return specs


def _cast_bf16(arrays):
    for a in arrays:
        assert a.ndim == 2 and a.shape[0] % (CAST_STEPS * BF16_SUBLANES) == 0 and a.shape[1] % LANES == 0
    specs = [pl.BlockSpec((a.shape[0] // CAST_STEPS, a.shape[1]), lambda i: (i, 0)) for a in arrays]
    return pl.pallas_call(
        _cast_kernel,
        out_shape=tuple(jax.ShapeDtypeStruct(a.shape, BF16) for a in arrays),
        grid=(CAST_STEPS,),
        in_specs=specs,
        out_specs=tuple(specs),
        compiler_params=_params("arbitrary"),
        name="cast_weights",
    )(*arrays)


def _mem_kv_kernel(mem_ref, g_ref, wk_ref, wv_ref, wq_ref, wco_ref, k_ref, v_ref, qk_ref, vo_ref):
    mn = _rmsnorm(mem_ref[0], g_ref[...]).astype(BF16)
    k = _dot(mn, wk_ref[...])
    v = _dot(mn, wv_ref[...])
    kt = k.T.astype(BF16)
    vb = v.astype(BF16)
    for hd in range(MEM_HEADS):
        sl = slice(hd * MEM_HEAD_DIM, (hd + 1) * MEM_HEAD_DIM)
        mem = slice(hd * N_MEM, (hd + 1) * N_MEM)
        k_ref[0, 0, :, hd, :] = k[:, sl]
        v_ref[0, 0, :, hd, :] = v[:, sl]
        qk_ref[0, :, mem] = (_dot(wq_ref[:, sl], kt[sl, :]) * (MEM_HEAD_DIM ** -0.5)).astype(BF16)
        vo_ref[0, mem, :] = _dot(vb[:, sl], wco_ref[sl, :]).astype(BF16)


def _mem_kv(mem, g, wk, wv, wq, wco):
    nb = mem.shape[0]
    blk = pl.BlockSpec((1, N_MEM, D_MODEL), lambda b: (b, 0, 0))
    blk_qk = pl.BlockSpec((1, D_MODEL, MEM_HEADS * N_MEM), lambda b: (b, 0, 0))
    blk_vo = pl.BlockSpec((1, MEM_HEADS * N_MEM, D_MODEL), lambda b: (b, 0, 0))
    blk_heads = pl.BlockSpec((1, 1, N_MEM, MEM_HEADS, MEM_HEAD_DIM), lambda b: (0, b, 0, 0, 0))
    square = _const_spec((D_MODEL, D_MODEL))
    return pl.pallas_call(
        _mem_kv_kernel,
        out_shape=(jax.ShapeDtypeStruct((1, nb, N_MEM, MEM_HEADS, MEM_HEAD_DIM), F32),) * 2
        + (jax.ShapeDtypeStruct((nb, D_MODEL, MEM_HEADS * N_MEM), BF16),
           jax.ShapeDtypeStruct((nb, MEM_HEADS * N_MEM, D_MODEL), BF16)),
        grid=(nb,),
        in_specs=[blk, _const_spec((1, D_MODEL)), square, square, square, square],
        out_specs=(blk_heads, blk_heads, blk_qk, blk_vo),
        compiler_params=_params("arbitrary"),
        name="mem_kv",
    )(mem, g, wk, wv, wq, wco)


def _pool_fold_kernel(wpool_ref, pscale_ref, wpa_ref, out_ref):
    for g in range(POOL_GROUPS):
        sl = slice(g * POOL_GW, (g + 1) * POOL_GW)
        scaled = (wpool_ref[g] * pscale_ref[:, sl]).astype(BF16)
        out_ref[sl, :] = _dot(scaled, wpa_ref[sl, :].astype(BF16)).astype(BF16)


def _pool_fold(w_pool, pool_scale, w_pa):
    args = [w_pool, pool_scale, w_pa]
    return pl.pallas_call(
        _pool_fold_kernel,
        out_shape=jax.ShapeDtypeStruct((D_POOL, D_MODEL), BF16),
        grid=(1,),
        in_specs=[_const_spec(a.shape) for a in args],
        out_specs=_const_spec((D_POOL, D_MODEL)),
        compiler_params=_params("arbitrary"),
        name="pool_fold",
    )(*args)


def _run_side_by_side(stage_generators):
    live = list(stage_generators)
    while live:
        live = [g for g in live if next(g, StopIteration) is not StopIteration]


def _ffn_kernel(*refs, final, sub, n_cast, tail, tail_pre):
    refs = list(refs)
    x_ref, g_ref, w1_ref, w3_ref, w2_ref = refs[:5]
    refs = refs[5:]
    gf_ref = refs.pop(0) if final else None
    tail_x_ref = refs.pop(0) if tail else None
    tail_o_ref, wco_ref = (refs.pop(0), refs.pop(0)) if tail_pre else (None, None)
    cast_in, refs = refs[:n_cast], refs[n_cast:]
    out_ref = refs.pop(0)
    tail_out_ref = refs.pop(0) if tail else None
    cast_out = refs
    _cast_kernel(*cast_in, *cast_out)

    def sub_tile(x_ref, o_ref, out_ref, r0):
        rows = slice(r0, r0 + sub)
        x = x_ref[rows, :]
        if o_ref is not None:
            x = x + _dot(o_ref[rows, :], wco_ref[...])
        n = _rmsnorm(x, g_ref[...]).astype(BF16)
        yield
        a = _dot(n, w1_ref[...])
        yield
        b = _dot(n, w3_ref[...])
        yield
        mid = (jax.nn.silu(a) * b).astype(BF16)
        y = x + 0.5 * _dot(mid, w2_ref[...])
        if final:
            y = _rmsnorm(y, gf_ref[...])
        out_ref[rows, :] = y

    def tile(x_ref, o_ref, out_ref):
        starts = list(range(0, x_ref.shape[0], sub))
        for i in range(0, len(starts), FFN_SIDE_BY_SIDE):
            _run_side_by_side([sub_tile(x_ref, o_ref, out_ref, r0) for r0 in starts[i:i + FFN_SIDE_BY_SIDE]])

    tile(x_ref, None, out_ref)

    if tail:
        @pl.when(pl.program_id(0) == pl.num_programs(0) - 1)
        def _():
            tile(tail_x_ref, tail_o_ref, tail_out_ref)


def _ffn(x, g, w1, w3, w2, *, g_final=None, tail_x=None, tail_o=None, wco=None, side_cast=(), name):
    rows = x.shape[0]
    tm = min(FFN_ROWS, rows)
    sub = min(FFN_SUB_ROWS, tm)
    assert rows % tm == 0 and tm % sub == 0
    final, tail, tail_pre = g_final is not None, tail_x is not None, tail_o is not None
    row_spec = pl.BlockSpec((tm, D_MODEL), lambda i: (i, 0))
    args = [x, g, w1, w3, w2]
    specs = [row_spec, _const_spec((1, D_MODEL)), _const_spec((D_MODEL, D_FF)), _const_spec((D_MODEL, D_FF)),
             _const_spec((D_FF, D_MODEL))]
    if final:
        args.append(g_final)
        specs.append(_const_spec((1, D_MODEL)))
    out_shapes, out_specs = [jax.ShapeDtypeStruct((rows, D_MODEL), F32)], [row_spec]
    if tail:
        assert tail_x.shape[0] % sub == 0
        args.append(tail_x)
        specs.append(_const_spec(tail_x.shape))
        out_shapes.append(jax.ShapeDtypeStruct(tail_x.shape, F32))
        out_specs.append(_const_spec(tail_x.shape))
    if tail_pre:
        args += [tail_o, wco]
        specs += [_const_spec(tail_o.shape), _const_spec((D_MODEL, D_MODEL))]
    cast_specs = _side_cast_specs(side_cast, rows // tm, lambda i: i)
    outs = pl.pallas_call(
        functools.partial(_ffn_kernel, final=final, sub=sub, n_cast=len(side_cast), tail=tail, tail_pre=tail_pre),
        out_shape=tuple(out_shapes) + tuple(jax.ShapeDtypeStruct(a.shape, BF16) for a in side_cast),
        grid=(rows // tm,),
        in_specs=specs + cast_specs,
        out_specs=tuple(out_specs) + tuple(cast_specs),
        compiler_params=_params("arbitrary"),
        name=name,
    )(*args, *side_cast)
    n_main = len(out_shapes)
    return outs[0], (outs[1] if tail else None), tuple(outs[n_main:])


def _in_proj(h, g_mix, w_in, b_gate):
    n = _rmsnorm(h, g_mix).astype(BF16)
    z = _dot(n, w_in)
    xa = z[:, :D_POOL]
    uv = jax.nn.gelu(z[:, D_POOL:D_POOL + 2 * D_SGU])
    gate = jax.nn.sigmoid(z[:, D_POOL + 2 * D_SGU:] + b_gate)
    return xa, uv[:, :D_SGU], uv[:, D_SGU:], gate[:, :D_MODEL], gate[:, D_MODEL:]


def _pool_project(pooled, xa, wpool_ref, pool_scale):
    mixed = []
    for g in range(POOL_GROUPS):
        sl = slice(g * POOL_GW, (g + 1) * POOL_GW)
        d = (pooled[g] - xa[:, sl]).astype(BF16)
        mixed.append(_dot(d, wpool_ref[g]))
    return jnp.concatenate(mixed, axis=-1) * pool_scale


def _merge(h, a, us, g_a, g_b, wpa_ref, wpb_ref, wo_ref):
    merged = g_a * _dot(a.astype(BF16), wpa_ref[...]) + g_b * _dot(us.astype(BF16), wpb_ref[...])
    return h + _dot(merged.astype(BF16), wo_ref[...])


def _softmax(sc):
    e = jnp.exp(sc - jnp.max(sc, axis=-1, keepdims=True))
    return e / jnp.sum(e, axis=-1, keepdims=True)


def _window_sums(halo, xa_g, window):
    s = jnp.concatenate([halo, xa_g], axis=0)
    step = 1
    while step < window:
        s = s + pltpu.roll(s, step, axis=0)
        step *= 2
    return s[POOL_HALO:]


def _mixer_prompt_kernel(h_ref, qk_ref, vo_ref, gmix_ref, win_ref, bgate_ref, wpoolpa_ref,
                         gsgu_ref, ws_ref, bst_ref, wpb_ref, wo_ref, gca_ref,
                         *rest, tm, sub, n_cast):
    cast_in, (out_ref, pool_ref), cast_out = rest[:n_cast], rest[n_cast:n_cast + 2], rest[n_cast + 2:-1]
    carry_ref = rest[-1]
    _cast_kernel(*cast_in, *cast_out)
    j = pl.program_id(1)

    @pl.when(j == 0)
    def _():
        carry_ref[...] = jnp.zeros((POOL_HALO, D_POOL), F32)

    blk_r = lax.broadcasted_iota(jnp.int32, (SGU_CHUNK, SGU_CHUNK), 0) // CHUNK
    blk_c = lax.broadcasted_iota(jnp.int32, (SGU_CHUNK, SGU_CHUNK), 1) // CHUNK
    wm = [jnp.where(blk_r >= blk_c, ws_ref[hd], 0.0).astype(BF16) for hd in range(SGU_HEADS)]
    n_chunks = sub // SGU_CHUNK

    n_sub = tm // sub
    halos = [carry_ref[...]] + [None] * n_sub

    def sub_tile(i):
        r0 = i * sub
        h = h_ref[0, r0:r0 + sub, :]
        xa, u, v, g_a, g_b = _in_proj(h, gmix_ref[...], win_ref[...], bgate_ref[...])
        halos[i + 1] = xa[sub - POOL_HALO:, :]
        yield

        pos = j * tm + r0 + lax.broadcasted_iota(jnp.int32, (sub, 1), 0)
        d = []
        for g, w in enumerate(POOL_WINDOWS):
            sl = slice(g * POOL_GW, (g + 1) * POOL_GW)
            cnt = jnp.minimum(w, pos + 1).astype(F32)
            d.append(_window_sums(halos[i][:, sl], xa[:, sl], w) / cnt - xa[:, sl])
        a_pa = _dot(jnp.concatenate(d, axis=-1).astype(BF16), wpoolpa_ref[...])
        yield

        vn = _layernorm(v, gsgu_ref[...]).astype(BF16)
        s_heads = []
        for hd in range(SGU_HEADS):
            cols = slice(hd * SGU_HW, (hd + 1) * SGU_HW)
            vcat = jnp.concatenate([vn[c * SGU_CHUNK:(c + 1) * SGU_CHUNK, cols] for c in range(n_chunks)],
                                   axis=-1)
            sh = _dot(wm[hd], vcat) + bst_ref[:, hd:hd + 1]
            s_heads.append(jnp.concatenate([sh[:, c * SGU_HW:(c + 1) * SGU_HW] for c in range(n_chunks)],
                                           axis=0))
        s = jnp.concatenate(s_heads, axis=-1)
        yield

        merged = g_a * a_pa + g_b * _dot((u * s).astype(BF16), wpb_ref[...])
        h = h + _dot(merged.astype(BF16), wo_ref[...])
        yield
        sc = _dot(_rmsnorm(h, gca_ref[...]).astype(BF16), qk_ref[0])
        yield
        p = jnp.concatenate([_softmax(sc[:, hd * N_MEM:(hd + 1) * N_MEM]) for hd in range(MEM_HEADS)],
                            axis=-1).astype(BF16)
        yield
        out_ref[0, r0:r0 + sub, :] = h + _dot(p, vo_ref[0])

    _run_side_by_side([sub_tile(i) for i in range(n_sub)])

    carry_ref[...] = halos[n_sub]

    @pl.when(j == pl.num_programs(1) - 1)
    def _():
        pool_ref[0] = halos[n_sub]


def _mixer_prompt(h, qk, vo, w, side_cast=()):
    nb, seq, _ = h.shape
    tm, sub = MIX_ROWS, MIX_SUB_ROWS
    assert seq % tm == 0 and tm % sub == 0 and sub % SGU_CHUNK == 0
    row_spec = pl.BlockSpec((1, tm, D_MODEL), lambda b, j: (b, j, 0))
    kt_spec = pl.BlockSpec((1, D_MODEL, MEM_HEADS * N_MEM), lambda b, j: (b, 0, 0))
    v_spec = pl.BlockSpec((1, MEM_HEADS * N_MEM, D_MODEL), lambda b, j: (b, 0, 0))
    consts = [w["g_mix"], w["w_in"], w["b_gate"], w["w_pool_pa"], w["g_sgu"], w["w_s"],
              w["b_s_t"], w["w_pb"], w["w_o"], w["g_ca"]]
    tiles = seq // tm
    cast_specs = _side_cast_specs(side_cast, nb * tiles, lambda b, j: b * tiles + j)
    return pl.pallas_call(
        functools.partial(_mixer_prompt_kernel, tm=tm, sub=sub, n_cast=len(side_cast)),
        out_shape=(jax.ShapeDtypeStruct((nb, seq, D_MODEL), F32),
                   jax.ShapeDtypeStruct((nb, POOL_HALO, D_POOL), F32))
        + tuple(jax.ShapeDtypeStruct(a.shape, BF16) for a in side_cast),
        grid=(nb, tiles),
        in_specs=[row_spec, kt_spec, v_spec] + [_const_spec(c.shape) for c in consts] + cast_specs,
        out_specs=(row_spec, pl.BlockSpec((1, POOL_HALO, D_POOL), lambda b, j: (b, 0, 0))) + tuple(cast_specs),
        scratch_shapes=[pltpu.VMEM((POOL_HALO, D_POOL), F32)],
        compiler_params=_params("arbitrary", "arbitrary"),
        name="mixer_prompt",
    )(h, qk, vo, *consts, *side_cast)


def _mixer_sample_kernel(h_ref, state_ref, gmix_ref, win_ref, bgate_ref, wpool_ref, pscale_ref,
                         gsgu_ref, wsx_ref, bsx_ref, wpa_ref, wpb_ref, wo_ref, gca_ref, wq_ref,
                         out_ref, q_ref, pool_ref, vn_ref, *, nb, t, past_len):
    rows = t * nb
    halo = POOL_STATE * nb
    h = h_ref[...]
    xa, u, v, g_a, g_b = _in_proj(h, gmix_ref[...], win_ref[...], bgate_ref[...])

    cat = jnp.concatenate([state_ref[...], xa], axis=0)
    pooled = []
    for g, w in enumerate(POOL_WINDOWS):
        sl = slice(g * POOL_GW, (g + 1) * POOL_GW)
        acc = xa[:, sl]
        for back in range(1, w):
            acc = acc + cat[halo - back * nb:halo - back * nb + rows, sl]
        pos = past_len + lax.broadcasted_iota(jnp.int32, (rows, 1), 0) // nb
        cnt = jnp.minimum(w, pos + 1).astype(F32)
        pooled.append(acc / cnt)
    a = _pool_project(pooled, xa, wpool_ref, pscale_ref[...])
    pool_ref[...] = cat[rows:rows + halo, :]

    vn = _layernorm(v, gsgu_ref[...])
    vn_ref[...] = vn
    s_rows = []
    for p in range(t):
        acc = jnp.broadcast_to(bsx_ref[p:p + 1, :], (nb, D_SGU))
        for qq in range(t):
            if p // CHUNK >= qq // CHUNK:
                acc = acc + wsx_ref[p, qq:qq + 1, :] * vn[qq * nb:(qq + 1) * nb, :]
        s_rows.append(acc)
    s = jnp.concatenate(s_rows, axis=0)

    h = _merge(h, a, u * s, g_a, g_b, wpa_ref, wpb_ref, wo_ref)
    out_ref[...] = h
    q_ref[...] = _dot(_rmsnorm(h, gca_ref[...]).astype(BF16), wq_ref[...]).astype(BF16)


def _mixer_sample(h, state, w, *, nb, t, past_len):
    rows = nb * t
    consts = [w["g_mix"], w["w_in"], w["b_gate"], w["w_pool"], w["pool_scale"], w["g_sgu"], w["w_s_x"],
              w["b_s_x"], w["w_pa"], w["w_pb"], w["w_o"], w["g_ca"], w["w_q"]]
    args = [h, state] + consts
    return pl.pallas_call(
        functools.partial(_mixer_sample_kernel, nb=nb, t=t, past_len=past_len),
        out_shape=(jax.ShapeDtypeStruct((rows, D_MODEL), F32),
                   jax.ShapeDtypeStruct((rows, D_MODEL), BF16),
                   jax.ShapeDtypeStruct((POOL_STATE * nb, D_POOL), F32),
                   jax.ShapeDtypeStruct((rows, D_SGU), F32)),
        grid=(1,),
        in_specs=[_const_spec(a.shape) for a in args],
        out_specs=(_const_spec((rows, D_MODEL)), _const_spec((rows, D_MODEL)),
                   _const_spec((POOL_STATE * nb, D_POOL)), _const_spec((rows, D_SGU))),
        compiler_params=_params("arbitrary"),
        name="mixer_sample",
    )(*args)


def _attn_sample_kernel(q_ref, k_ref, v_ref, o_ref, *, streams):
    t = q_ref.shape[1]
    n_cols = N_MEM * MEM_HEADS
    row_head = lax.broadcasted_iota(jnp.int32, (MEM_HEADS * t, n_cols), 0) // t
    col_head = lax.broadcasted_iota(jnp.int32, (MEM_HEADS * t, n_cols), 1) % MEM_HEADS
    own_head = row_head == col_head

    def stream(s):
        q = q_ref[s]
        q_rows = jnp.concatenate([q[:, hd * MEM_HEAD_DIM:(hd + 1) * MEM_HEAD_DIM] for hd in range(MEM_HEADS)],
                                 axis=0)
        k = k_ref[0, s].reshape(n_cols, MEM_HEAD_DIM).astype(BF16)
        v = v_ref[0, s].reshape(n_cols, MEM_HEAD_DIM).astype(BF16)
        yield
        sc = lax.dot_general(q_rows, k, (((1,), (1,)), ((), ())), preferred_element_type=F32)
        sc = jnp.where(own_head, sc * (MEM_HEAD_DIM ** -0.5), -jnp.inf)
        yield
        p = _softmax(sc)
        yield
        o = _dot(p.astype(BF16), v)
        o_ref[s] = jnp.concatenate([o[hd * t:(hd + 1) * t, :] for hd in range(MEM_HEADS)],
                                   axis=-1).astype(BF16)

    _run_side_by_side([stream(s) for s in range(streams)])


def _attn_sample(q, k, v):
    nb, t, _ = q.shape
    streams = ATTN_SAMPLE_STREAMS
    assert nb % streams == 0
    q_spec = pl.BlockSpec((streams, t, D_MODEL), lambda b: (b, 0, 0))
    kv_spec = pl.BlockSpec((1, streams, N_MEM, MEM_HEADS, MEM_HEAD_DIM), lambda b: (0, b, 0, 0, 0))
    return pl.pallas_call(
        functools.partial(_attn_sample_kernel, streams=streams),
        out_shape=jax.ShapeDtypeStruct((nb, t, D_MODEL), BF16),
        grid=(nb // streams,),
        in_specs=[q_spec, kv_spec, kv_spec],
        out_specs=q_spec,
        compiler_params=_params("arbitrary"),
        name="attn_sample",
    )(q, k, v)


def kernel(x_prompt, x_sample, state_pool, cache_mem_k, cache_mem_v, mem_prompt, g_ff1, w1a, w3a, w2a,
           g_mix, w_in, b_gate, w_pool, pool_scale, g_sgu, w_s, b_s, w_pa, w_pb, w_o, g_mem, w_mk, w_mv,
           g_ca, w_q, w_co, g_ff2, w1b, w3b, w2b, g_final):
    nb, seq, _ = x_prompt.shape
    nbs, t, _ = x_sample.shape
    depth = g_ff1.shape[0]
    assert depth == 1
    l = 0
    row = lambda a: a.reshape(1, -1)
    mat = lambda a: a[l].reshape(-1, a.shape[-1])
    w1a_b, w3a_b, w2a_b = _cast_bf16([mat(a) for a in (w1a, w3a, w2a)])

    xs = jnp.transpose(x_sample, (1, 0, 2)).reshape(t * nbs, D_MODEL)
    st = jnp.transpose(state_pool[l], (1, 0, 2)).reshape(POOL_STATE * nbs, D_POOL)
    to_stream_major = lambda a, n: jnp.transpose(a.reshape(n, nbs, a.shape[-1]), (1, 0, 2))

    hp, hs, (w_in_b, w_pa_b, w_pb_b, w_o_b, w_q_b, w_co_b, w_pool_b, w_mk_b, w_mv_b) = _ffn(
        x_prompt.reshape(nb * seq, D_MODEL), row(g_ff1[l]), w1a_b, w3a_b, w2a_b, tail_x=xs,
        side_cast=[mat(a) for a in (w_in, w_pa, w_pb, w_o, w_q, w_co, w_pool, w_mk, w_mv)], name="ffn1")
    mk, mv, qk, vo = _mem_kv(mem_prompt, row(g_mem[l]), w_mk_b, w_mv_b, w_q_b, w_co_b)
    w = dict(
        g_mix=row(g_mix[l]), w_in=w_in_b, b_gate=row(b_gate[l]),
        w_pool=w_pool_b.reshape(POOL_GROUPS, POOL_GW, POOL_GW),
        w_pool_pa=_pool_fold(w_pool[l], row(pool_scale[l]), w_pa[l]),
        pool_scale=row(pool_scale[l]), g_sgu=row(g_sgu[l]), w_s=w_s[l], b_s_t=b_s[l].T,
        w_pa=w_pa_b, w_pb=w_pb_b, w_o=w_o_b, g_ca=row(g_ca[l]), w_q=w_q_b,
        w_s_x=jnp.repeat(jnp.transpose(w_s[l][:, :t, :t], (1, 2, 0)), SGU_HW, axis=-1),
        b_s_x=jnp.repeat(b_s[l][:, :t].T, SGU_HW, axis=-1),
    )

    hs, qs, pool_s, vn_s = _mixer_sample(hs, st, w, nb=nbs, t=t, past_len=PAST_LEN)
    hs = to_stream_major(hs, t).reshape(nbs * t, D_MODEL)
    os_ = _attn_sample(to_stream_major(qs, t), cache_mem_k, cache_mem_v).reshape(nbs * t, D_MODEL)
    hp, pool_p, w1b_b, w3b_b, w2b_b = _mixer_prompt(hp.reshape(nb, seq, D_MODEL), qk, vo, w,
                                                    side_cast=[mat(a) for a in (w1b, w3b, w2b)])

    y_prompt, y_sample, _ = _ffn(hp.reshape(nb * seq, D_MODEL), row(g_ff2[l]), w1b_b, w3b_b, w2b_b,
                                 g_final=row(g_final), tail_x=hs, tail_o=os_, wco=w_co_b, name="ffn2")

    pool_prompt = pool_p[:, POOL_HALO - POOL_STATE:, :][None]
    pool_sample = to_stream_major(pool_s, POOL_STATE)[None]
    sgu_v_sample = to_stream_major(vn_s, t)[None]
    return (y_prompt.reshape(nb, seq, D_MODEL), y_sample.reshape(nbs, t, D_MODEL), pool_prompt, pool_sample,
            sgu_v_sample, mk, mv)
```

```python
import functools

import jax
import jax.numpy as jnp
from jax import lax
from jax.experimental import pallas as pl
from jax.experimental.pallas import tpu as pltpu

D_MODEL = 1024
PAST_LEN = 1024
CHUNK = 64
N_MEM = 256
MEM_HEADS = 4
MEM_HEAD_DIM = D_MODEL // MEM_HEADS
D_POOL = D_MODEL // 2
POOL_WINDOWS = (2, 4, 8, 16)
POOL_GROUPS = len(POOL_WINDOWS)
POOL_GW = D_POOL // POOL_GROUPS
POOL_STATE = max(POOL_WINDOWS) - 1
D_SGU = D_MODEL // 2
SGU_HEADS = 4
SGU_HW = D_SGU // SGU_HEADS
SGU_CHUNK = 128
D_FF = 2816
D_IN = D_POOL + 2 * D_SGU + 2 * D_MODEL
EPS = 1e-6

V7X_VMEM_LIMIT_BYTES = 56 * 1024 * 1024
F32_SUBLANES = 8
BF16_SUBLANES = 16
LANES = 128
CAST_STEPS = 16
POOL_HALO = 2 * F32_SUBLANES
assert POOL_HALO >= POOL_STATE + 1

FFN_ROWS = 1024
FFN_SUB_ROWS = 256
FFN_SIDE_BY_SIDE = 2
MIX_ROWS = 1024
MIX_SUB_ROWS = 256
ATTN_SAMPLE_STREAMS = 4

BF16 = jnp.bfloat16
F32 = jnp.float32


def _dot(a, b):
    return jnp.dot(a, b, preferred_element_type=F32)


def _rmsnorm(x, g):
    y = x * lax.rsqrt(jnp.mean(x * x, axis=-1, keepdims=True) + EPS)
    return y * g


def _layernorm(x, g):
    mu = jnp.mean(x, axis=-1, keepdims=True)
    xc = x - mu
    y = xc * lax.rsqrt(jnp.mean(xc * xc, axis=-1, keepdims=True) + EPS)
    return y * g


def _const_spec(shape):
    nd = len(shape)
    return pl.BlockSpec(shape, lambda *_: (0,) * nd, pipeline_mode=pl.Buffered(1))


def _params(*sem):
    return pltpu.CompilerParams(dimension_semantics=sem, vmem_limit_bytes=V7X_VMEM_LIMIT_BYTES)


def _cast_kernel(*refs):
    n = len(refs) // 2
    for src, dst in zip(refs[:n], refs[n:]):
        dst[...] = src[...].astype(BF16)


def _side_cast_specs(arrays, steps, flat_step):
    specs = []
    for a in arrays:
        rows = a.shape[0]
        n = next(n for n in range(steps, 0, -1) if rows % (n * BF16_SUBLANES) == 0)
        specs.append(pl.BlockSpec((rows // n, a.shape[1]),
                                  lambda *g, n=n: (jnp.minimum(flat_step(*g), n - 1), 0)))
    return specs


def _cast_bf16(arrays):
    for a in arrays:
        assert a.ndim == 2 and a.shape[0] % (CAST_STEPS * BF16_SUBLANES) == 0 and a.shape[1] % LANES == 0
    specs = [pl.BlockSpec((a.shape[0] // CAST_STEPS, a.shape[1]), lambda i: (i, 0)) for a in arrays]
    return pl.pallas_call(
        _cast_kernel,
        out_shape=tuple(jax.ShapeDtypeStruct(a.shape, BF16) for a in arrays),
        grid=(CAST_STEPS,),
        in_specs=specs,
        out_specs=tuple(specs),
        compiler_params=_params("arbitrary"),
        name="cast_weights",
    )(*arrays)


def _mem_kv_kernel(mem_ref, g_ref, wk_ref, wv_ref, wq_ref, wco_ref, k_ref, v_ref, qk_ref, vo_ref):
    mn = _rmsnorm(mem_ref[0], g_ref[...]).astype(BF16)
    k = _dot(mn, wk_ref[...])
    v = _dot(mn, wv_ref[...])
    kt = k.T.astype(BF16)
    vb = v.astype(BF16)
    for hd in range(MEM_HEADS):
        sl = slice(hd * MEM_HEAD_DIM, (hd + 1) * MEM_HEAD_DIM)
        mem = slice(hd * N_MEM, (hd + 1) * N_MEM)
        k_ref[0, 0, :, hd, :] = k[:, sl]
        v_ref[0, 0, :, hd, :] = v[:, sl]
        qk_ref[0, :, mem] = (_dot(wq_ref[:, sl], kt[sl, :]) * (MEM_HEAD_DIM ** -0.5)).astype(BF16)
        vo_ref[0, mem, :] = _dot(vb[:, sl], wco_ref[sl, :]).astype(BF16)


def _mem_kv(mem, g, wk, wv, wq, wco):
    nb = mem.shape[0]
    blk = pl.BlockSpec((1, N_MEM, D_MODEL), lambda b: (b, 0, 0))
    blk_qk = pl.BlockSpec((1, D_MODEL, MEM_HEADS * N_MEM), lambda b: (b, 0, 0))
    blk_vo = pl.BlockSpec((1, MEM_HEADS * N_MEM, D_MODEL), lambda b: (b, 0, 0))
    blk_heads = pl.BlockSpec((1, 1, N_MEM, MEM_HEADS, MEM_HEAD_DIM), lambda b: (0, b, 0, 0, 0))
    square = _const_spec((D_MODEL, D_MODEL))
    return pl.pallas_call(
        _mem_kv_kernel,
        out_shape=(jax.ShapeDtypeStruct((1, nb, N_MEM, MEM_HEADS, MEM_HEAD_DIM), F32),) * 2
        + (jax.ShapeDtypeStruct((nb, D_MODEL, MEM_HEADS * N_MEM), BF16),
           jax.ShapeDtypeStruct((nb, MEM_HEADS * N_MEM, D_MODEL), BF16)),
        grid=(nb,),
        in_specs=[blk, _const_spec((1, D_MODEL)), square, square, square, square],
        out_specs=(blk_heads, blk_heads, blk_qk, blk_vo),
        compiler_params=_params("arbitrary"),
        name="mem_kv",
    )(mem, g, wk, wv, wq, wco)


def _pool_fold_kernel(wpool_ref, pscale_ref, wpa_ref, out_ref):
    for g in range(POOL_GROUPS):
        sl = slice(g * POOL_GW, (g + 1) * POOL_GW)
        scaled = (wpool_ref[g] * pscale_ref[:, sl]).astype(BF16)
        out_ref[sl, :] = _dot(scaled, wpa_ref[sl, :].astype(BF16)).astype(BF16)


def _pool_fold(w_pool, pool_scale, w_pa):
    args = [w_pool, pool_scale, w_pa]
    return pl.pallas_call(
        _pool_fold_kernel,
        out_shape=jax.ShapeDtypeStruct((D_POOL, D_MODEL), BF16),
        grid=(1,),
        in_specs=[_const_spec(a.shape) for a in args],
        out_specs=_const_spec((D_POOL, D_MODEL)),
        compiler_params=_params("arbitrary"),
        name="pool_fold",
    )(*args)


def _run_side_by_side(stage_generators):
    live = list(stage_generators)
    while live:
        live = [g for g in live if next(g, StopIteration) is not StopIteration]


def _ffn_kernel(*refs, final, sub, n_cast, tail, tail_pre):
    refs = list(refs)
    x_ref, g_ref, w1_ref, w3_ref, w2_ref = refs[:5]
    refs = refs[5:]
    gf_ref = refs.pop(0) if final else None
    tail_x_ref = refs.pop(0) if tail else None
    tail_o_ref, wco_ref = (refs.pop(0), refs.pop(0)) if tail_pre else (None, None)
    cast_in, refs = refs[:n_cast], refs[n_cast:]
    out_ref = refs.pop(0)
    tail_out_ref = refs.pop(0) if tail else None
    cast_out = refs
    _cast_kernel(*cast_in, *cast_out)

    def sub_tile(x_ref, o_ref, out_ref, r0):
        rows = slice(r0, r0 + sub)
        x = x_ref[rows, :]
        if o_ref is not None:
            x = x + _dot(o_ref[rows, :], wco_ref[...])
        n = _rmsnorm(x, g_ref[...]).astype(BF16)
        yield
        a = _dot(n, w1_ref[...])
        yield
        b = _dot(n, w3_ref[...])
        yield
        mid = (jax.nn.silu(a) * b).astype(BF16)
        y = x + 0.5 * _dot(mid, w2_ref[...])
        if final:
            y = _rmsnorm(y, gf_ref[...])
        out_ref[rows, :] = y

    def tile(x_ref, o_ref, out_ref):
        starts = list(range(0, x_ref.shape[0], sub))
        for i in range(0, len(starts), FFN_SIDE_BY_SIDE):
            _run_side_by_side([sub_tile(x_ref, o_ref, out_ref, r0) for r0 in starts[i:i + FFN_SIDE_BY_SIDE]])

    tile(x_ref, None, out_ref)

    if tail:
        @pl.when(pl.program_id(0) == pl.num_programs(0) - 1)
        def _():
            tile(tail_x_ref, tail_o_ref, tail_out_ref)


def _ffn(x, g, w1, w3, w2, *, g_final=None, tail_x=None, tail_o=None, wco=None, side_cast=(), name):
    rows = x.shape[0]
    tm = min(FFN_ROWS, rows)
    sub = min(FFN_SUB_ROWS, tm)
    assert rows % tm == 0 and tm % sub == 0
    final, tail, tail_pre = g_final is not None, tail_x is not None, tail_o is not None
    row_spec = pl.BlockSpec((tm, D_MODEL), lambda i: (i, 0))
    args = [x, g, w1, w3, w2]
    specs = [row_spec, _const_spec((1, D_MODEL)), _const_spec((D_MODEL, D_FF)), _const_spec((D_MODEL, D_FF)),
             _const_spec((D_FF, D_MODEL))]
    if final:
        args.append(g_final)
        specs.append(_const_spec((1, D_MODEL)))
    out_shapes, out_specs = [jax.ShapeDtypeStruct((rows, D_MODEL), F32)], [row_spec]
    if tail:
        assert tail_x.shape[0] % sub == 0
        args.append(tail_x)
        specs.append(_const_spec(tail_x.shape))
        out_shapes.append(jax.ShapeDtypeStruct(tail_x.shape, F32))
        out_specs.append(_const_spec(tail_x.shape))
    if tail_pre:
        args += [tail_o, wco]
        specs += [_const_spec(tail_o.shape), _const_spec((D_MODEL, D_MODEL))]
    cast_specs = _side_cast_specs(side_cast, rows // tm, lambda i: i)
    outs = pl.pallas_call(
        functools.partial(_ffn_kernel, final=final, sub=sub, n_cast=len(side_cast), tail=tail, tail_pre=tail_pre),
        out_shape=tuple(out_shapes) + tuple(jax.ShapeDtypeStruct(a.shape, BF16) for a in side_cast),
        grid=(rows // tm,),
        in_specs=specs + cast_specs,
        out_specs=tuple(out_specs) + tuple(cast_specs),
        compiler_params=_params("arbitrary"),
        name=name,
    )(*args, *side_cast)
    n_main = len(out_shapes)
    return outs[0], (outs[1] if tail else None), tuple(outs[n_main:])


def _in_proj(h, g_mix, w_in, b_gate):
    n = _rmsnorm(h, g_mix).astype(BF16)
    z = _dot(n, w_in)
    xa = z[:, :D_POOL]
    uv = jax.nn.gelu(z[:, D_POOL:D_POOL + 2 * D_SGU])
    gate = jax.nn.sigmoid(z[:, D_POOL + 2 * D_SGU:] + b_gate)
    return xa, uv[:, :D_SGU], uv[:, D_SGU:], gate[:, :D_MODEL], gate[:, D_MODEL:]


def _pool_project(pooled, xa, wpool_ref, pool_scale):
    mixed = []
    for g in range(POOL_GROUPS):
        sl = slice(g * POOL_GW, (g + 1) * POOL_GW)
        d = (pooled[g] - xa[:, sl]).astype(BF16)
        mixed.append(_dot(d, wpool_ref[g]))
    return jnp.concatenate(mixed, axis=-1) * pool_scale


def _merge(h, a, us, g_a, g_b, wpa_ref, wpb_ref, wo_ref):
    merged = g_a * _dot(a.astype(BF16), wpa_ref[...]) + g_b * _dot(us.astype(BF16), wpb_ref[...])
    return h + _dot(merged.astype(BF16), wo_ref[...])


def _softmax(sc):
    e = jnp.exp(sc - jnp.max(sc, axis=-1, keepdims=True))
    return e / jnp.sum(e, axis=-1, keepdims=True)


def _window_sums(halo, xa_g, window):
    s = jnp.concatenate([halo, xa_g], axis=0)
    step = 1
    while step < window:
        s = s + pltpu.roll(s, step, axis=0)
        step *= 2
    return s[POOL_HALO:]


def _mixer_prompt_kernel(h_ref, qk_ref, vo_ref, gmix_ref, win_ref, bgate_ref, wpoolpa_ref,
                         gsgu_ref, ws_ref, bst_ref, wpb_ref, wo_ref, gca_ref,
                         *rest, tm, sub, n_cast):
    cast_in, (out_ref, pool_ref), cast_out = rest[:n_cast], rest[n_cast:n_cast + 2], rest[n_cast + 2:-1]
    carry_ref = rest[-1]
    _cast_kernel(*cast_in, *cast_out)
    j = pl.program_id(1)

    @pl.when(j == 0)
    def _():
        carry_ref[...] = jnp.zeros((POOL_HALO, D_POOL), F32)

    blk_r = lax.broadcasted_iota(jnp.int32, (SGU_CHUNK, SGU_CHUNK), 0) // CHUNK
    blk_c = lax.broadcasted_iota(jnp.int32, (SGU_CHUNK, SGU_CHUNK), 1) // CHUNK
    wm = [jnp.where(blk_r >= blk_c, ws_ref[hd], 0.0).astype(BF16) for hd in range(SGU_HEADS)]
    n_chunks = sub // SGU_CHUNK

    n_sub = tm // sub
    halos = [carry_ref[...]] + [None] * n_sub

    def sub_tile(i):
        r0 = i * sub
        h = h_ref[0, r0:r0 + sub, :]
        xa, u, v, g_a, g_b = _in_proj(h, gmix_ref[...], win_ref[...], bgate_ref[...])
        halos[i + 1] = xa[sub - POOL_HALO:, :]
        yield

        pos = j * tm + r0 + lax.broadcasted_iota(jnp.int32, (sub, 1), 0)
        d = []
        for g, w in enumerate(POOL_WINDOWS):
            sl = slice(g * POOL_GW, (g + 1) * POOL_GW)
            cnt = jnp.minimum(w, pos + 1).astype(F32)
            d.append(_window_sums(halos[i][:, sl], xa[:, sl], w) / cnt - xa[:, sl])
        a_pa = _dot(jnp.concatenate(d, axis=-1).astype(BF16), wpoolpa_ref[...])
        yield

        vn = _layernorm(v, gsgu_ref[...]).astype(BF16)
        s_heads = []
        for hd in range(SGU_HEADS):
            cols = slice(hd * SGU_HW, (hd + 1) * SGU_HW)
            vcat = jnp.concatenate([vn[c * SGU_CHUNK:(c + 1) * SGU_CHUNK, cols] for c in range(n_chunks)],
                                   axis=-1)
            sh = _dot(wm[hd], vcat) + bst_ref[:, hd:hd + 1]
            s_heads.append(jnp.concatenate([sh[:, c * SGU_HW:(c + 1) * SGU_HW] for c in range(n_chunks)],
                                           axis=0))
        s = jnp.concatenate(s_heads, axis=-1)
        yield

        merged = g_a * a_pa + g_b * _dot((u * s).astype(BF16), wpb_ref[...])
        h = h + _dot(merged.astype(BF16), wo_ref[...])
        yield
        sc = _dot(_rmsnorm(h, gca_ref[...]).astype(BF16), qk_ref[0])
        yield
        p = jnp.concatenate([_softmax(sc[:, hd * N_MEM:(hd + 1) * N_MEM]) for hd in range(MEM_HEADS)],
                            axis=-1).astype(BF16)
        yield
        out_ref[0, r0:r0 + sub, :] = h + _dot(p, vo_ref[0])

    _run_side_by_side([sub_tile(i) for i in range(n_sub)])

    carry_ref[...] = halos[n_sub]

    @pl.when(j == pl.num_programs(1) - 1)
    def _():
        pool_ref[0] = halos[n_sub]


def _mixer_prompt(h, qk, vo, w, side_cast=()):
    nb, seq, _ = h.shape
    tm, sub = MIX_ROWS, MIX_SUB_ROWS
    assert seq % tm == 0 and tm % sub == 0 and sub % SGU_CHUNK == 0
    row_spec = pl.BlockSpec((1, tm, D_MODEL), lambda b, j: (b, j, 0))
    kt_spec = pl.BlockSpec((1, D_MODEL, MEM_HEADS * N_MEM), lambda b, j: (b, 0, 0))
    v_spec = pl.BlockSpec((1, MEM_HEADS * N_MEM, D_MODEL), lambda b, j: (b, 0, 0))
    consts = [w["g_mix"], w["w_in"], w["b_gate"], w["w_pool_pa"], w["g_sgu"], w["w_s"],
              w["b_s_t"], w["w_pb"], w["w_o"], w["g_ca"]]
    tiles = seq // tm
    cast_specs = _side_cast_specs(side_cast, nb * tiles, lambda b, j: b * tiles + j)
    return pl.pallas_call(
        functools.partial(_mixer_prompt_kernel, tm=tm, sub=sub, n_cast=len(side_cast)),
        out_shape=(jax.ShapeDtypeStruct((nb, seq, D_MODEL), F32),
                   jax.ShapeDtypeStruct((nb, POOL_HALO, D_POOL), F32))
        + tuple(jax.ShapeDtypeStruct(a.shape, BF16) for a in side_cast),
        grid=(nb, tiles),
        in_specs=[row_spec, kt_spec, v_spec] + [_const_spec(c.shape) for c in consts] + cast_specs,
        out_specs=(row_spec, pl.BlockSpec((1, POOL_HALO, D_POOL), lambda b, j: (b, 0, 0))) + tuple(cast_specs),
        scratch_shapes=[pltpu.VMEM((POOL_HALO, D_POOL), F32)],
        compiler_params=_params("arbitrary", "arbitrary"),
        name="mixer_prompt",
    )(h, qk, vo, *consts, *side_cast)


def _mixer_sample_kernel(h_ref, state_ref, gmix_ref, win_ref, bgate_ref, wpool_ref, pscale_ref,
                         gsgu_ref, wsx_ref, bsx_ref, wpa_ref, wpb_ref, wo_ref, gca_ref, wq_ref,
                         out_ref, q_ref, pool_ref, vn_ref, *, nb, t, past_len):
    rows = t * nb
    halo = POOL_STATE * nb

    def store_stream_major(ref, value):
        for i in range(t):
            ref[:, i, :] = value[i * nb:(i + 1) * nb, :]

    h = jnp.concatenate([h_ref[:, i, :] for i in range(t)], axis=0)
    xa, u, v, g_a, g_b = _in_proj(h, gmix_ref[...], win_ref[...], bgate_ref[...])

    state = [state_ref[:, i, :] for i in range(POOL_STATE)]
    cat = jnp.concatenate(state + [xa], axis=0)
    pooled = []
    for g, w in enumerate(POOL_WINDOWS):
        sl = slice(g * POOL_GW, (g + 1) * POOL_GW)
        acc = xa[:, sl]
        for back in range(1, w):
            acc = acc + cat[halo - back * nb:halo - back * nb + rows, sl]
        pos = past_len + lax.broadcasted_iota(jnp.int32, (rows, 1), 0) // nb
        cnt = jnp.minimum(w, pos + 1).astype(F32)
        pooled.append(acc / cnt)
    a = _pool_project(pooled, xa, wpool_ref, pscale_ref[...])
    for i in range(POOL_STATE):
        pool_ref[:, i, :] = cat[rows + i * nb:rows + (i + 1) * nb, :]

    vn = _layernorm(v, gsgu_ref[...])
    store_stream_major(vn_ref, vn)
    s_rows = []
    for p in range(t):
        acc = jnp.broadcast_to(bsx_ref[p:p + 1, :], (nb, D_SGU))
        for qq in range(t):
            if p // CHUNK >= qq // CHUNK:
                acc = acc + wsx_ref[p, qq:qq + 1, :] * vn[qq * nb:(qq + 1) * nb, :]
        s_rows.append(acc)
    s = jnp.concatenate(s_rows, axis=0)

    h = _merge(h, a, u * s, g_a, g_b, wpa_ref, wpb_ref, wo_ref)
    store_stream_major(out_ref, h)
    store_stream_major(q_ref, _dot(_rmsnorm(h, gca_ref[...]).astype(BF16), wq_ref[...]))


def _mixer_sample(h, state, w, *, nb, t, past_len):
    consts = [w["g_mix"], w["w_in"], w["b_gate"], w["w_pool"], w["pool_scale"], w["g_sgu"], w["w_s_x"],
              w["b_s_x"], w["w_pa"], w["w_pb"], w["w_o"], w["g_ca"], w["w_q"]]
    args = [h, state] + consts
    return pl.pallas_call(
        functools.partial(_mixer_sample_kernel, nb=nb, t=t, past_len=past_len),
        out_shape=(jax.ShapeDtypeStruct((nb, t, D_MODEL), F32),
                   jax.ShapeDtypeStruct((nb, t, D_MODEL), F32),
                   jax.ShapeDtypeStruct((nb, POOL_STATE, D_POOL), F32),
                   jax.ShapeDtypeStruct((nb, t, D_SGU), F32)),
        grid=(1,),
        in_specs=[_const_spec(a.shape) for a in args],
        out_specs=(_const_spec((nb, t, D_MODEL)), _const_spec((nb, t, D_MODEL)),
                   _const_spec((nb, POOL_STATE, D_POOL)), _const_spec((nb, t, D_SGU))),
        compiler_params=_params("arbitrary"),
        name="mixer_sample",
    )(*args)


def _attn_sample_kernel(q_ref, k_ref, v_ref, o_ref, *, streams):
    t = q_ref.shape[1]
    n_cols = N_MEM * MEM_HEADS
    row_head = lax.broadcasted_iota(jnp.int32, (MEM_HEADS * t, n_cols), 0) // t
    col_head = lax.broadcasted_iota(jnp.int32, (MEM_HEADS * t, n_cols), 1) % MEM_HEADS
    own_head = row_head == col_head

    def stream(s):
        q = q_ref[s].astype(BF16)
        q_rows = jnp.concatenate([q[:, hd * MEM_HEAD_DIM:(hd + 1) * MEM_HEAD_DIM] for hd in range(MEM_HEADS)],
                                 axis=0)
        k = k_ref[0, s].reshape(n_cols, MEM_HEAD_DIM).astype(BF16)
        v = v_ref[0, s].reshape(n_cols, MEM_HEAD_DIM).astype(BF16)
        yield
        sc = lax.dot_general(q_rows, k, (((1,), (1,)), ((), ())), preferred_element_type=F32)
        sc = jnp.where(own_head, sc * (MEM_HEAD_DIM ** -0.5), -jnp.inf)
        yield
        p = _softmax(sc)
        yield
        o = _dot(p.astype(BF16), v)
        o_ref[s] = jnp.concatenate([o[hd * t:(hd + 1) * t, :] for hd in range(MEM_HEADS)],
                                   axis=-1).astype(BF16)

    _run_side_by_side([stream(s) for s in range(streams)])


def _attn_sample(q, k, v):
    nb, t, _ = q.shape
    streams = ATTN_SAMPLE_STREAMS
    assert nb % streams == 0
    q_spec = pl.BlockSpec((streams, t, D_MODEL), lambda b: (b, 0, 0))
    kv_spec = pl.BlockSpec((1, streams, N_MEM, MEM_HEADS, MEM_HEAD_DIM), lambda b: (0, b, 0, 0, 0))
    return pl.pallas_call(
        functools.partial(_attn_sample_kernel, streams=streams),
        out_shape=jax.ShapeDtypeStruct((nb, t, D_MODEL), BF16),
        grid=(nb // streams,),
        in_specs=[q_spec, kv_spec, kv_spec],
        out_specs=q_spec,
        compiler_params=_params("arbitrary"),
        name="attn_sample",
    )(q, k, v)


def kernel(x_prompt, x_sample, state_pool, cache_mem_k, cache_mem_v, mem_prompt, g_ff1, w1a, w3a, w2a,
           g_mix, w_in, b_gate, w_pool, pool_scale, g_sgu, w_s, b_s, w_pa, w_pb, w_o, g_mem, w_mk, w_mv,
           g_ca, w_q, w_co, g_ff2, w1b, w3b, w2b, g_final):
    nb, seq, _ = x_prompt.shape
    nbs, t, _ = x_sample.shape
    depth = g_ff1.shape[0]
    assert depth == 1
    l = 0
    row = lambda a: a.reshape(1, -1)
    mat = lambda a: a[l].reshape(-1, a.shape[-1])
    w1a_b, w3a_b, w2a_b = _cast_bf16([mat(a) for a in (w1a, w3a, w2a)])

    xs = x_sample.reshape(nbs * t, D_MODEL)

    hp, hs, (w_in_b, w_pa_b, w_pb_b, w_o_b, w_q_b, w_co_b, w_pool_b, w_mk_b, w_mv_b) = _ffn(
        x_prompt.reshape(nb * seq, D_MODEL), row(g_ff1[l]), w1a_b, w3a_b, w2a_b, tail_x=xs,
        side_cast=[mat(a) for a in (w_in, w_pa, w_pb, w_o, w_q, w_co, w_pool, w_mk, w_mv)], name="ffn1")
    mk, mv, qk, vo = _mem_kv(mem_prompt, row(g_mem[l]), w_mk_b, w_mv_b, w_q_b, w_co_b)
    w = dict(
        g_mix=row(g_mix[l]), w_in=w_in_b, b_gate=row(b_gate[l]),
        w_pool=w_pool_b.reshape(POOL_GROUPS, POOL_GW, POOL_GW),
        w_pool_pa=_pool_fold(w_pool[l], row(pool_scale[l]), w_pa[l]),
        pool_scale=row(pool_scale[l]), g_sgu=row(g_sgu[l]), w_s=w_s[l], b_s_t=b_s[l].T,
        w_pa=w_pa_b, w_pb=w_pb_b, w_o=w_o_b, g_ca=row(g_ca[l]), w_q=w_q_b,
        w_s_x=jnp.repeat(jnp.transpose(w_s[l][:, :t, :t], (1, 2, 0)), SGU_HW, axis=-1),
        b_s_x=jnp.repeat(b_s[l][:, :t].T, SGU_HW, axis=-1),
    )

    hs, qs, pool_s, vn_s = _mixer_sample(hs.reshape(nbs, t, D_MODEL), state_pool[l], w, nb=nbs, t=t,
                                         past_len=PAST_LEN)
    hs = hs.reshape(nbs * t, D_MODEL)
    os_ = _attn_sample(qs, cache_mem_k, cache_mem_v).reshape(nbs * t, D_MODEL)
    hp, pool_p, w1b_b, w3b_b, w2b_b = _mixer_prompt(hp.reshape(nb, seq, D_MODEL), qk, vo, w,
                                                    side_cast=[mat(a) for a in (w1b, w3b, w2b)])

    y_prompt, y_sample, _ = _ffn(hp.reshape(nb * seq, D_MODEL), row(g_ff2[l]), w1b_b, w3b_b, w2b_b,
                                 g_final=row(g_final), tail_x=hs, tail_o=os_, wco=w_co_b, name="ffn2")

    pool_prompt = pool_p[:, POOL_HALO - POOL_STATE:, :][None]
    pool_sample = pool_s[None]
    sgu_v_sample = vn_s[None]
    return (y_prompt.reshape(nb, seq, D_MODEL), y_sample.reshape(nbs, t, D_MODEL), pool_prompt, pool_sample,
            sgu_v_sample, mk, mv)
```
